```python
import jax, jax.numpy as jnp
from jax import lax
import numpy as np

D_MODEL = 1024
BATCH = 8
SEQ = 4096
DEPTH = 4

D_MIX = D_MODEL
D_POOL = D_MIX // 2
N_POOL_GROUPS = 4
POOL_GROUP = D_POOL // N_POOL_GROUPS
POOL_WINDOWS = (2, 4, 8, 16)
N_MLA_HEADS = 4
QK_NOPE = 128
QK_ROPE = 64
V_HEAD = 128
D_MLA = N_MLA_HEADS * V_HEAD
Q_LORA = 256
KV_LORA = 128
D_IN = D_POOL + Q_LORA + KV_LORA + QK_ROPE
ROPE_THETA = 10000.0
Q_BLOCK = 128
N_MEM = 256
N_XHEADS = 4
XHEAD = D_MODEL // N_XHEADS
N_EXPERTS = 16
CAPACITY_FACTOR = 2
D_EXPERT = 2 * D_MODEL
EPS = 1e-6

kernel_name = "hybrid_pool_mla_memxattn_ecmoe_encoder"


def rmsnorm(t, gain):
    t32 = t.astype(jnp.float32)
    y = t32 * lax.rsqrt(jnp.mean(t32 * t32, axis=-1, keepdims=True) + EPS) * gain.astype(jnp.float32)
    return y.astype(t.dtype)


def apply_rope(t, cos, sin):
    t1, t2 = jnp.split(t.astype(jnp.float32), 2, axis=-1)
    return jnp.concatenate([t1 * cos - t2 * sin, t1 * sin + t2 * cos], axis=-1).astype(t.dtype)


def pool_mixer(u, w_pool, pool_scale):
    B, S, _ = u.shape
    u32 = u.astype(jnp.float32)
    cs = jnp.concatenate([jnp.zeros((B, 1, D_POOL), jnp.float32), jnp.cumsum(u32, axis=1)], axis=1)
    t = jnp.arange(S)
    outs = []
    for g, w in enumerate(POOL_WINDOWS):
        left = w // 2
        right = w - 1 - left
        lo = jnp.clip(t - left, 0, S)
        hi = jnp.clip(t + right + 1, 0, S)
        sl = slice(g * POOL_GROUP, (g + 1) * POOL_GROUP)
        csg = cs[:, :, sl]
        mean = (csg[:, hi] - csg[:, lo]) / (hi - lo).astype(jnp.float32)[:, None]
        outs.append(jnp.einsum('bsc,cd->bsd', (mean - u32[:, :, sl]).astype(u.dtype), w_pool[g]))
    return jnp.concatenate(outs, axis=-1) * pool_scale


def mla_attention(q_nope, q_rope, k_nope, k_rope, v):
    B, S, H, _ = q_nope.shape
    nb = S // Q_BLOCK
    scale = (QK_NOPE + QK_ROPE) ** -0.5

    def blocks(t):
        return jnp.moveaxis(t.reshape(B, nb, Q_BLOCK, *t.shape[2:]), 1, 0)

    def one_block(args):
        qn, qr = args
        s = jnp.einsum('bqhd,bkhd->bhqk', qn, k_nope) + jnp.einsum('bqhr,bkr->bhqk', qr, k_rope)
        p = jax.nn.softmax(s.astype(jnp.float32) * scale, axis=-1).astype(v.dtype)
        return jnp.einsum('bhqk,bkhd->bqhd', p, v)

    o = lax.map(one_block, (blocks(q_nope), blocks(q_rope)))
    return jnp.moveaxis(o, 0, 1).reshape(B, S, H * V_HEAD)


def memory_cross_attention(h, m, wq, wkv, wo):
    B, S, _ = h.shape
    M = m.shape[1]
    q = (h @ wq).reshape(B, S, N_XHEADS, XHEAD)
    k, v = jnp.split(m @ wkv, 2, axis=-1)
    k = k.reshape(B, M, N_XHEADS, XHEAD)
    v = v.reshape(B, M, N_XHEADS, XHEAD)
    s = jnp.einsum('bqhd,bkhd->bhqk', q, k).astype(jnp.float32) * (XHEAD ** -0.5)
    p = jax.nn.softmax(s, axis=-1).astype(v.dtype)
    o = jnp.einsum('bhqk,bkhd->bqhd', p, v).reshape(B, S, D_MODEL)
    return o @ wo


def expert_choice_ffn(h, w_router, w_gate, w_up, w_down):
    B, S, D = h.shape
    cap = CAPACITY_FACTOR * S // N_EXPERTS
    affinity = jax.nn.softmax(h.astype(jnp.float32) @ w_router.astype(jnp.float32), axis=-1)
    gates, idx = lax.top_k(jnp.swapaxes(affinity, 1, 2), cap)
    b_idx = jnp.arange(B)[:, None, None]
    xt = h[b_idx, idx]
    a = jnp.einsum('becd,edf->becf', xt, w_gate)
    g = jnp.einsum('becd,edf->becf', xt, w_up)
    y = jnp.einsum('becf,efd->becd', jax.nn.silu(a) * g, w_down) * gates[..., None].astype(h.dtype)
    return jnp.zeros_like(h).at[b_idx, idx].add(y)


def setup_inputs(seed: int = 0) -> dict:
    key = jax.random.key(seed)
    ks = jax.random.split(key, 24)
    L = DEPTH

    def w(k, shape, fan_in):
        return jax.random.normal(k, shape, jnp.float32) * (fan_in ** -0.5)

    def gain(k, shape):
        return 1.0 + 0.05 * jax.random.normal(k, shape, jnp.float32)

    x = jax.random.normal(ks[0], (BATCH, SEQ, D_MODEL), jnp.float32)
    mem = jax.random.normal(ks[1], (BATCH, N_MEM, D_MODEL), jnp.float32)
    offsets = jax.random.randint(ks[2], (BATCH, 1), 0, 1024, dtype=jnp.int32)
    positions = offsets + jnp.arange(SEQ, dtype=jnp.int32)[None, :]
    return {
        "x": x,
        "mem": mem,
        "positions": positions,
        "norm_mix": gain(ks[3], (L, D_MODEL)),
        "w_in": w(ks[4], (L, D_MODEL, D_IN), D_MODEL),
        "q_norm": gain(ks[5], (L, Q_LORA)),
        "kv_norm": gain(ks[6], (L, KV_LORA)),
        "w_uq": w(ks[7], (L, Q_LORA, N_MLA_HEADS * (QK_NOPE + QK_ROPE)), Q_LORA),
        "w_ukv": w(ks[8], (L, KV_LORA, N_MLA_HEADS * (QK_NOPE + V_HEAD)), KV_LORA),
        "w_pool": w(ks[9], (L, N_POOL_GROUPS, POOL_GROUP, POOL_GROUP), POOL_GROUP),
        "pool_scale": gain(ks[10], (L, D_POOL)),
        "w_out": w(ks[11], (L, D_MIX, D_MODEL), D_MIX),
        "norm_x": gain(ks[12], (L, D_MODEL)),
        "mem_norm": gain(ks[13], (L, D_MODEL)),
        "wx_q": w(ks[14], (L, D_MODEL, D_MODEL), D_MODEL),
        "wx_kv": w(ks[15], (L, D_MODEL, 2 * D_MODEL), D_MODEL),
        "wx_o": w(ks[16], (L, D_MODEL, D_MODEL), D_MODEL),
        "norm_ffn": gain(ks[17], (L, D_MODEL)),
        "w_router": w(ks[18], (L, D_MODEL, N_EXPERTS), D_MODEL),
        "w_gate": w(ks[19], (L, N_EXPERTS, D_MODEL, D_EXPERT), D_MODEL),
        "w_up": w(ks[20], (L, N_EXPERTS, D_MODEL, D_EXPERT), D_MODEL),
        "w_down": w(ks[21], (L, N_EXPERTS, D_EXPERT, D_MODEL), D_EXPERT),
        "norm_final": gain(ks[22], (D_MODEL,)),
    }


def reference(x, mem, positions, norm_mix, w_in, q_norm, kv_norm, w_uq, w_ukv, w_pool, pool_scale,
              w_out, norm_x, mem_norm, wx_q, wx_kv, wx_o, norm_ffn, w_router, w_gate, w_up, w_down,
              norm_final):
    B, S, _ = x.shape
    half = QK_ROPE // 2
    inv_freq = ROPE_THETA ** (-jnp.arange(half, dtype=jnp.float32) / half)
    ang = positions.astype(jnp.float32)[..., None] * inv_freq
    cos, sin = jnp.cos(ang), jnp.sin(ang)
    splits = [D_POOL, D_POOL + Q_LORA, D_POOL + Q_LORA + KV_LORA]

    for l in range(DEPTH):
        h = rmsnorm(x, norm_mix[l])
        proj = h @ w_in[l]
        u, c_q, c_kv, k_r = jnp.split(proj, splits, axis=-1)
        pool_out = pool_mixer(u, w_pool[l], pool_scale[l])
        q = (rmsnorm(c_q, q_norm[l]) @ w_uq[l]).reshape(B, S, N_MLA_HEADS, QK_NOPE + QK_ROPE)
        q_nope, q_rope = jnp.split(q, [QK_NOPE], axis=-1)
        kv = (rmsnorm(c_kv, kv_norm[l]) @ w_ukv[l]).reshape(B, S, N_MLA_HEADS, QK_NOPE + V_HEAD)
        k_nope, v = jnp.split(kv, [QK_NOPE], axis=-1)
        q_rope = apply_rope(q_rope, cos[:, :, None, :], sin[:, :, None, :])
        k_rope = apply_rope(k_r, cos, sin)
        mla_out = mla_attention(q_nope, q_rope, k_nope, k_rope, v)
        x = x + jnp.concatenate([pool_out, mla_out], axis=-1) @ w_out[l]

        h = rmsnorm(x, norm_x[l])
        m = rmsnorm(mem, mem_norm[l])
        x = x + memory_cross_attention(h, m, wx_q[l], wx_kv[l], wx_o[l])

        h = rmsnorm(x, norm_ffn[l])
        x = x + expert_choice_ffn(h, w_router[l], w_gate[l], w_up[l], w_down[l])

    return rmsnorm(x, norm_final)
```

```python
import functools

import jax
import jax.numpy as jnp
from jax import lax
from jax.experimental import pallas as pl
from jax.experimental.pallas import tpu as pltpu

F32 = jnp.float32
BF16 = jnp.bfloat16

D_MODEL = 1024
D_POOL = 512
N_POOL_GROUPS = 4
POOL_GROUP = 128
POOL_WINDOWS = (2, 4, 8, 16)
POOL_HALO = 8
N_MLA_HEADS = 4
QK_NOPE = 128
QK_ROPE = 64
V_HEAD = 128
Q_LORA = 256
KV_LORA = 128
D_IN = D_POOL + Q_LORA + KV_LORA + QK_ROPE
ROPE_THETA = 10000.0
N_XHEADS = 4
XHEAD = D_MODEL // N_XHEADS
N_EXPERTS = 16
CAPACITY_FACTOR = 2
D_EXPERT = 2 * D_MODEL
EPS = 1e-6

LANES = 128
QK_PAD = 256
D_IN_EXT = D_POOL + Q_LORA + KV_LORA + 2 * LANES
Q_HEAD_EXT = QK_NOPE + 2 * LANES
ROUTE_TILE = 256
FFN_CHUNK = 512
V7X_VMEM_LIMIT = 56 * 1024 * 1024
NEG_BIG = -1e30


def _cparams(*sem):
    return pltpu.CompilerParams(dimension_semantics=sem, vmem_limit_bytes=V7X_VMEM_LIMIT)


def _rms(t, gain):
    return t * lax.rsqrt(jnp.mean(t * t, axis=-1, keepdims=True) + EPS) * gain


def _dot(a, b):
    return jnp.dot(a, b, preferred_element_type=F32)


def _dot_nt(a, b):
    return lax.dot_general(a, b, (((1,), (1,)), ((), ())), preferred_element_type=F32)


def _memkv_kernel(mem_ref, gain_ref, w_ref, o_ref):
    h = _rms(mem_ref[0], gain_ref[0])
    o_ref[0, 0] = _dot(h.astype(BF16), w_ref[0]).astype(BF16)


def _memkv(mem, mem_norm, wx_kv):
    B, M, D = mem.shape
    L = wx_kv.shape[0]
    return pl.pallas_call(
        _memkv_kernel,
        grid=(L, B),
        in_specs=[
            pl.BlockSpec((1, M, D), lambda l, b: (b, 0, 0)),
            pl.BlockSpec((1, 1, D), lambda l, b: (l, 0, 0)),
            pl.BlockSpec((1, D, 2 * D), lambda l, b: (l, 0, 0)),
        ],
        out_specs=pl.BlockSpec((1, 1, M, 2 * D), lambda l, b: (l, b, 0, 0)),
        out_shape=jax.ShapeDtypeStruct((L, B, M, 2 * D), BF16),
        compiler_params=_cparams("parallel", "parallel"),
        name="memkv",
    )(mem, mem_norm, wx_kv)


def _mix_in_kernel(x_ref, g_ref, win_ref, qn_ref, kvn_ref, wuq_ref, wukv_ref, cos_ref, sin_ref,
                   u_ref, q_ref, k_ref, v_ref):
    h = _rms(x_ref[0], g_ref[0]).astype(BF16)
    proj = _dot(h, win_ref[0])
    u_ref[0] = proj[:, :D_POOL]
    cos = cos_ref[0]
    sin = sin_ref[0]
    o = D_POOL
    cq = _rms(proj[:, o:o + Q_LORA], qn_ref[0]).astype(BF16)
    o += Q_LORA
    ckv = _rms(proj[:, o:o + KV_LORA], kvn_ref[0]).astype(BF16)
    o += KV_LORA
    k_rope = (proj[:, o:o + LANES] * cos + proj[:, o + LANES:o + 2 * LANES] * sin).astype(BF16)
    qall = _dot(cq, wuq_ref[0])
    kvall = _dot(ckv, wukv_ref[0])
    scale = (QK_NOPE + QK_ROPE) ** -0.5
    for hd in range(N_MLA_HEADS):
        qb = hd * Q_HEAD_EXT
        q_ref[0, hd, :, :QK_NOPE] = (qall[:, qb:qb + QK_NOPE] * scale).astype(BF16)
        roped = qall[:, qb + QK_NOPE:qb + QK_NOPE + LANES] * cos + qall[:, qb + QK_NOPE + LANES:qb + Q_HEAD_EXT] * sin
        q_ref[0, hd, :, QK_NOPE:] = (roped * scale).astype(BF16)
        kb = hd * (QK_NOPE + V_HEAD)
        k_ref[0, hd, :, :QK_NOPE] = kvall[:, kb:kb + QK_NOPE].astype(BF16)
        k_ref[0, hd, :, QK_NOPE:] = k_rope
        v_ref[0, hd] = kvall[:, kb + QK_NOPE:kb + QK_NOPE + V_HEAD].astype(BF16)


def _mix_in(x, gain, w_in_ext, q_norm, kv_norm, w_uq_ext, w_ukv, cos, sin, l, tm):
    B, S, D = x.shape
    H = N_MLA_HEADS
    lmap = lambda b, i: (l, 0, 0)
    tmap = lambda b, i: (b, i, 0)
    hmap = lambda b, i: (b, 0, i, 0)
    return pl.pallas_call(
        _mix_in_kernel,
        grid=(B, S // tm),
        in_specs=[
            pl.BlockSpec((1, tm, D), tmap),
            pl.BlockSpec((1, 1, D), lmap),
            pl.BlockSpec((1, D, D_IN_EXT), lmap),
            pl.BlockSpec((1, 1, Q_LORA), lmap),
            pl.BlockSpec((1, 1, KV_LORA), lmap),
            pl.BlockSpec((1, Q_LORA, H * Q_HEAD_EXT), lmap),
            pl.BlockSpec((1, KV_LORA, H * (QK_NOPE + V_HEAD)), lmap),
            pl.BlockSpec((1, tm, LANES), tmap),
            pl.BlockSpec((1, tm, LANES), tmap),
        ],
        out_specs=[
            pl.BlockSpec((1, tm, D_POOL), tmap),
            pl.BlockSpec((1, H, tm, QK_PAD), hmap),
            pl.BlockSpec((1, H, tm, QK_PAD), hmap),
            pl.BlockSpec((1, H, tm, V_HEAD), hmap),
        ],
        out_shape=[
            jax.ShapeDtypeStruct((B, S, D_POOL), F32),
            jax.ShapeDtypeStruct((B, H, S, QK_PAD), BF16),
            jax.ShapeDtypeStruct((B, H, S, QK_PAD), BF16),
            jax.ShapeDtypeStruct((B, H, S, V_HEAD), BF16),
        ],
        compiler_params=_cparams("parallel", "parallel"),
        name="mix_in",
    )(x, gain, w_in_ext, q_norm, kv_norm, w_uq_ext, w_ukv, cos, sin)


def _mla_kernel(q_ref, k_ref, v_ref, o_ref):
    s = _dot_nt(q_ref[0, 0], k_ref[0, 0])
    m = jnp.max(s, axis=-1, keepdims=True)
    p = jnp.exp(s - m)
    denom = jnp.sum(p, axis=-1, keepdims=True)
    o = _dot(p.astype(BF16), v_ref[0, 0])
    o_ref[0] = (o / denom).astype(BF16)


def _mla_attn(q, k, v, tq):
    B, H, S, _ = q.shape
    return pl.pallas_call(
        _mla_kernel,
        grid=(B, H, S // tq),
        in_specs=[
            pl.BlockSpec((1, 1, tq, QK_PAD), lambda b, h, i: (b, h, i, 0)),
            pl.BlockSpec((1, 1, S, QK_PAD), lambda b, h, i: (b, h, 0, 0)),
            pl.BlockSpec((1, 1, S, V_HEAD), lambda b, h, i: (b, h, 0, 0)),
        ],
        out_specs=pl.BlockSpec((1, tq, V_HEAD), lambda b, h, i: (b, i, h)),
        out_shape=jax.ShapeDtypeStruct((B, S, H * V_HEAD), BF16),
        compiler_params=_cparams("parallel", "parallel", "parallel"),
        name="mla_attn",
    )(q, k, v)


def _mix_out_kernel(x_ref, u_ref, up_ref, un_ref, wpool_ref, ps_ref, mla_ref, wout_ref,
                    nx_ref, wxq_ref, kv_ref, wxo_ref, nf_ref, wr_ref,
                    x2_ref, hn_ref, aff_ref, ext_ref, *, S, tm):
    i = pl.program_id(1)
    nt = pl.num_programs(1)
    hal = POOL_HALO
    ext_ref[0:hal, :] = jnp.where(i > 0, up_ref[0], 0.0)
    ext_ref[hal:hal + tm, :] = u_ref[0]
    ext_ref[hal + tm:2 * hal + tm, :] = jnp.where(i < nt - 1, un_ref[0], 0.0)
    t = i * tm + lax.broadcasted_iota(jnp.int32, (tm, POOL_GROUP), 0)
    pooled = []
    for g, w in enumerate(POOL_WINDOWS):
        left = w // 2
        right = w - 1 - left
        lanes = slice(g * POOL_GROUP, (g + 1) * POOL_GROUP)
        acc = ext_ref[pl.ds(hal - left, tm), lanes]
        for d in range(-left + 1, right + 1):
            acc = acc + ext_ref[pl.ds(hal + d, tm), lanes]
        cnt = (jnp.clip(t + right + 1, 0, S) - jnp.clip(t - left, 0, S)).astype(F32)
        diff = (acc / cnt - ext_ref[pl.ds(hal, tm), lanes]).astype(BF16)
        pooled.append(_dot(diff, wpool_ref[0, g]) * ps_ref[0, :, lanes])
    pool_out = jnp.concatenate(pooled, axis=-1).astype(BF16)
    x1 = x_ref[0] + _dot(pool_out, wout_ref[0, :D_POOL, :]) + _dot(mla_ref[0], wout_ref[0, D_POOL:, :])

    h = _rms(x1, nx_ref[0]).astype(BF16)
    q = (_dot(h, wxq_ref[0]) * (XHEAD ** -0.5)).astype(BF16)
    outs = []
    for hd in range(N_XHEADS):
        cols = slice(hd * XHEAD, (hd + 1) * XHEAD)
        kh = kv_ref[0, 0, :, cols]
        vh = kv_ref[0, 0, :, D_MODEL + hd * XHEAD:D_MODEL + (hd + 1) * XHEAD]
        s = _dot_nt(q[:, cols], kh)
        p = jnp.exp(s - jnp.max(s, axis=-1, keepdims=True))
        denom = jnp.sum(p, axis=-1, keepdims=True)
        outs.append((_dot(p.astype(BF16), vh) / denom).astype(BF16))
    x2 = x1 + _dot(jnp.concatenate(outs, axis=-1), wxo_ref[0])
    x2_ref[0] = x2

    hn = _rms(x2, nf_ref[0])
    hn_ref[0] = hn.astype(BF16)
    logits = jnp.dot(hn, wr_ref[0], precision=lax.Precision.HIGHEST, preferred_element_type=F32)
    lane = lax.broadcasted_iota(jnp.int32, logits.shape, 1)
    logits = jnp.where(lane < N_EXPERTS, logits, NEG_BIG)
    pe = jnp.exp(logits - jnp.max(logits, axis=-1, keepdims=True))
    aff_ref[0] = pe / jnp.sum(pe, axis=-1, keepdims=True)


def _mix_out(x, u, w_pool, pool_scale, mla, w_out, norm_x, wx_q, kvmem, wx_o, norm_ffn, w_router, l, tm):
    B, S, D = x.shape
    M = kvmem.shape[2]
    nh = tm // POOL_HALO
    last = S // POOL_HALO - 1
    lmap = lambda b, i: (l, 0, 0)
    tmap = lambda b, i: (b, i, 0)
    return pl.pallas_call(
        functools.partial(_mix_out_kernel, S=S, tm=tm),
        grid=(B, S // tm),
        in_specs=[
            pl.BlockSpec((1, tm, D), tmap),
            pl.BlockSpec((1, tm, D_POOL), tmap),
            pl.BlockSpec((1, POOL_HALO, D_POOL), lambda b, i: (b, jnp.maximum(i * nh - 1, 0), 0)),
            pl.BlockSpec((1, POOL_HALO, D_POOL), lambda b, i: (b, jnp.minimum((i + 1) * nh, last), 0)),
            pl.BlockSpec((1, N_POOL_GROUPS, POOL_GROUP, POOL_GROUP), lambda b, i: (l, 0, 0, 0)),
            pl.BlockSpec((1, 1, D_POOL), lmap),
            pl.BlockSpec((1, tm, D_POOL), tmap),
            pl.BlockSpec((1, D, D), lmap),
            pl.BlockSpec((1, 1, D), lmap),
            pl.BlockSpec((1, D, D), lmap),
            pl.BlockSpec((1, 1, M, 2 * D), lambda b, i: (l, b, 0, 0)),
            pl.BlockSpec((1, D, D), lmap),
            pl.BlockSpec((1, 1, D), lmap),
            pl.BlockSpec((1, D, LANES), lmap),
        ],
        out_specs=[
            pl.BlockSpec((1, tm, D), tmap),
            pl.BlockSpec((1, tm, D), tmap),
            pl.BlockSpec((1, tm, LANES), tmap),
        ],
        out_shape=[
            jax.ShapeDtypeStruct((B, S, D), F32),
            jax.ShapeDtypeStruct((B, S, D), BF16),
            jax.ShapeDtypeStruct((B, S, LANES), F32),
        ],
        scratch_shapes=[pltpu.VMEM((tm + 2 * POOL_HALO, D_POOL), F32)],
        compiler_params=_cparams("parallel", "parallel"),
        name="mix_out",
    )(x, u, u, u, w_pool, pool_scale, mla, w_out, norm_x, wx_q, kvmem, wx_o, norm_ffn, w_router)


def _select_kernel(aff_ref, pos_ref, post_ref, cnt_ref, *, S, C):
    rt = ROUTE_TILE
    n_tiles = S // rt
    bits = lax.bitcast_convert_type(aff_ref[0], jnp.int32)

    def search(it, prefix):
        cand = prefix | jnp.left_shift(jnp.int32(1), 30 - it)
        cnt = jnp.sum(jnp.where(bits >= cand, 1.0, 0.0), axis=0, keepdims=True)
        return jnp.where(cnt >= C, cand, prefix)

    thr = lax.fori_loop(0, 31, search, jnp.zeros((1, LANES), jnp.int32))
    n_gt = jnp.sum(jnp.where(bits > thr, 1.0, 0.0), axis=0, keepdims=True)
    need = C - n_gt
    r = lax.broadcasted_iota(jnp.int32, (rt, rt), 0)
    c = lax.broadcasted_iota(jnp.int32, (rt, rt), 1)
    tri = jnp.where(c <= r, 1.0, 0.0).astype(BF16)
    cnt_ref[...] = jnp.zeros(cnt_ref.shape, jnp.int32)
    off_eq = jnp.zeros((1, LANES), F32)
    off_sel = jnp.zeros((1, LANES), F32)
    for kt in range(n_tiles):
        rows = slice(kt * rt, (kt + 1) * rt)
        bc = bits[rows]
        gt = bc > thr
        eq = bc == thr
        eqf = jnp.where(eq, 1.0, 0.0)
        rank = off_eq + _dot(tri, eqf.astype(BF16)) - eqf
        sel = gt | (eq & (rank < need))
        self_ = jnp.where(sel, 1.0, 0.0)
        slot = off_sel + _dot(tri, self_.astype(BF16)) - self_
        posc = jnp.where(sel, slot, -1.0)
        pos_ref[0, rows, :] = posc
        post_ref[0, :, rows] = posc.T
        cnt_ref[0, kt:kt + 1, :] = off_sel.astype(jnp.int32)
        off_eq = off_eq + jnp.sum(eqf, axis=0, keepdims=True)
        off_sel = off_sel + jnp.sum(self_, axis=0, keepdims=True)
    cnt_ref[0, n_tiles:n_tiles + 1, :] = off_sel.astype(jnp.int32)


def _select(aff, C):
    B, S, _ = aff.shape
    n_rows = -(-(S // ROUTE_TILE + 1) // 8) * 8
    return pl.pallas_call(
        functools.partial(_select_kernel, S=S, C=C),
        grid=(B,),
        in_specs=[pl.BlockSpec((1, S, LANES), lambda b: (b, 0, 0))],
        out_specs=[
            pl.BlockSpec((1, S, LANES), lambda b: (b, 0, 0)),
            pl.BlockSpec((1, LANES, S), lambda b: (b, 0, 0)),
            pl.BlockSpec((1, n_rows, LANES), lambda b: (b, 0, 0)),
        ],
        out_shape=[
            jax.ShapeDtypeStruct((B, S, LANES), F32),
            jax.ShapeDtypeStruct((B, LANES, S), F32),
            jax.ShapeDtypeStruct((B, n_rows, LANES), jnp.int32),
        ],
        compiler_params=_cparams("parallel"),
        name="select",
    )(aff)


def _overlap(cnt_ref, base, t, j):
    lo = cnt_ref[base + t]
    hi = cnt_ref[base + t + 1]
    return (hi > lo) & (hi > j * ROUTE_TILE) & (lo < (j + 1) * ROUTE_TILE)


def _moe_ffn_kernel(cnt_ref, post_ref, hn_ref, wg_ref, wu_ref, wd_ref, y_ref, xg_ref, *, S, C):
    rt = ROUTE_TILE
    n_tiles = S // rt
    e = pl.program_id(0)
    b = pl.program_id(1)
    base = (b * N_EXPERTS + e) * (n_tiles + 1)
    xg_ref[...] = jnp.zeros(xg_ref.shape, F32)
    slot = lax.broadcasted_iota(jnp.int32, (rt, rt), 0).astype(F32)
    for j in range(C // rt):
        for t in range(n_tiles):
            @pl.when(_overlap(cnt_ref, base, t, j))
            def _():
                prow = post_ref[0, 0, :, t * rt:(t + 1) * rt]
                onehot = jnp.where(prow == slot + float(j * rt), 1.0, 0.0).astype(BF16)
                xg_ref[j * rt:(j + 1) * rt, :] += _dot(onehot, hn_ref[0, t * rt:(t + 1) * rt, :])
    x = xg_ref[...].astype(BF16)
    y = jnp.zeros((C, D_MODEL), F32)
    for c in range(D_EXPERT // FFN_CHUNK):
        cols = slice(c * FFN_CHUNK, (c + 1) * FFN_CHUNK)
        a = _dot(x, wg_ref[0, 0, :, cols])
        g = _dot(x, wu_ref[0, 0, :, cols])
        y = y + _dot((jax.nn.silu(a) * g).astype(BF16), wd_ref[0, 0, cols, :])
    y_ref[0, 0] = y.astype(BF16)


def _moe_ffn(cnt, post, hn, w_gate, w_up, w_down, l, C):
    B, S, D = hn.shape
    E = N_EXPERTS
    return pl.pallas_call(
        functools.partial(_moe_ffn_kernel, S=S, C=C),
        grid_spec=pltpu.PrefetchScalarGridSpec(
            num_scalar_prefetch=1,
            grid=(E, B),
            in_specs=[
                pl.BlockSpec((1, 1, 1, S), lambda e, b, cnt: (b, e, 0, 0)),
                pl.BlockSpec((1, S, D), lambda e, b, cnt: (b, 0, 0)),
                pl.BlockSpec((1, 1, D, D_EXPERT), lambda e, b, cnt: (l, e, 0, 0)),
                pl.BlockSpec((1, 1, D, D_EXPERT), lambda e, b, cnt: (l, e, 0, 0)),
                pl.BlockSpec((1, 1, D_EXPERT, D), lambda e, b, cnt: (l, e, 0, 0)),
            ],
            out_specs=pl.BlockSpec((1, 1, C, D), lambda e, b, cnt: (b, e, 0, 0)),
            scratch_shapes=[pltpu.VMEM((C, D), F32)],
        ),
        out_shape=jax.ShapeDtypeStruct((B, E, C, D), BF16),
        compiler_params=_cparams("arbitrary", "arbitrary"),
        name="moe_ffn",
    )(cnt, post, hn, w_gate, w_up, w_down)


def _combine_kernel(cnt_ref, x_ref, y_ref, pos_ref, aff_ref, gf_ref, o_ref, *, S, C, tt, final):
    rt = ROUTE_TILE
    n_tiles = S // rt
    b = pl.program_id(0)
    t4 = pl.program_id(1)
    e = pl.program_id(2)
    base = (b * N_EXPERTS + e) * (n_tiles + 1)

    @pl.when(e == 0)
    def _():
        o_ref[0] = x_ref[0]

    lane = lax.broadcasted_iota(jnp.int32, (rt, LANES), 1)
    slot = lax.broadcasted_iota(jnp.int32, (rt, rt), 1).astype(F32)
    for sub in range(tt // rt):
        rows = slice(sub * rt, (sub + 1) * rt)
        for j in range(C // rt):
            @pl.when(_overlap(cnt_ref, base, t4 * (tt // rt) + sub, j))
            def _():
                mine = lane == e
                pcol = jnp.sum(jnp.where(mine, pos_ref[0, rows, :], 0.0), axis=-1, keepdims=True)
                gate = jnp.sum(jnp.where(mine, aff_ref[0, rows, :], 0.0), axis=-1, keepdims=True)
                onehot = jnp.where(pcol == slot + float(j * rt), 1.0, 0.0).astype(BF16)
                o_ref[0, rows, :] += gate * _dot(onehot, y_ref[0, 0, j * rt:(j + 1) * rt, :])

    if final:
        @pl.when(e == N_EXPERTS - 1)
        def _():
            o_ref[0] = _rms(o_ref[0], gf_ref[0])


def _combine(cnt, x, y, pos, aff, norm_final, C, tt, final):
    B, S, D = x.shape
    E = N_EXPERTS
    return pl.pallas_call(
        functools.partial(_combine_kernel, S=S, C=C, tt=tt, final=final),
        grid_spec=pltpu.PrefetchScalarGridSpec(
            num_scalar_prefetch=1,
            grid=(B, S // tt, E),
            in_specs=[
                pl.BlockSpec((1, tt, D), lambda b, t, e, cnt: (b, t, 0)),
                pl.BlockSpec((1, 1, C, D), lambda b, t, e, cnt: (b, e, 0, 0)),
                pl.BlockSpec((1, tt, LANES), lambda b, t, e, cnt: (b, t, 0)),
                pl.BlockSpec((1, tt, LANES), lambda b, t, e, cnt: (b, t, 0)),
                pl.BlockSpec((1, 1, D), lambda b, t, e, cnt: (0, 0, 0)),
            ],
            out_specs=pl.BlockSpec((1, tt, D), lambda b, t, e, cnt: (b, t, 0)),
        ),
        out_shape=jax.ShapeDtypeStruct((B, S, D), F32),
        compiler_params=_cparams("parallel", "parallel", "arbitrary"),
        name="combine",
    )(cnt, x, y, pos, aff, norm_final)


def _rot_cols(w):
    half = w.shape[-1] // 2
    return jnp.concatenate([-w[..., half:], w[..., :half]], axis=-1)


def _pad_lanes(w):
    return jnp.concatenate([w, jnp.zeros(w.shape[:-1] + (LANES - w.shape[-1],), w.dtype)], axis=-1)


def kernel(x, mem, positions, norm_mix, w_in, q_norm, kv_norm, w_uq, w_ukv, w_pool, pool_scale,
           w_out, norm_x, mem_norm, wx_q, wx_kv, wx_o, norm_ffn, w_router, w_gate, w_up, w_down,
           norm_final):
    B, S, D = x.shape
    L = w_in.shape[0]
    C = CAPACITY_FACTOR * S // N_EXPERTS
    assert D == D_MODEL and S % 1024 == 0 and C % ROUTE_TILE == 0
    tm = 512
    tq = 256
    tt = 1024

    half = QK_ROPE // 2
    inv_freq = ROPE_THETA ** (-jnp.arange(half, dtype=F32) / half)
    ang = positions.astype(F32)[..., None] * inv_freq
    cos = _pad_lanes(jnp.concatenate([jnp.cos(ang)] * 2, axis=-1))
    sin = _pad_lanes(jnp.concatenate([jnp.sin(ang)] * 2, axis=-1))

    k_r = w_in[..., D_IN - QK_ROPE:]
    w_in_ext = jnp.concatenate(
        [w_in[..., :D_IN - QK_ROPE], _pad_lanes(k_r), _pad_lanes(_rot_cols(k_r))], axis=-1).astype(BF16)
    wq = w_uq.reshape(L, Q_LORA, N_MLA_HEADS, QK_NOPE + QK_ROPE)
    q_r = wq[..., QK_NOPE:]
    w_uq_ext = jnp.concatenate([wq[..., :QK_NOPE], _pad_lanes(q_r), _pad_lanes(_rot_cols(q_r))], axis=-1)
    w_uq_ext = w_uq_ext.reshape(L, Q_LORA, N_MLA_HEADS * Q_HEAD_EXT).astype(BF16)
    w_router_p = _pad_lanes(w_router)
    row = lambda g: g.reshape(L, 1, g.shape[-1])
    norm_mix, q_norm, kv_norm, pool_scale, norm_x, mem_norm, norm_ffn = map(
        row, (norm_mix, q_norm, kv_norm, pool_scale, norm_x, mem_norm, norm_ffn))
    norm_final = norm_final.reshape(1, 1, D)
    w_ukv, w_pool, w_out, wx_q, wx_kv, wx_o, w_gate, w_up, w_down = (
        w.astype(BF16) for w in (w_ukv, w_pool, w_out, wx_q, wx_kv, wx_o, w_gate, w_up, w_down))

    kvmem = _memkv(mem, mem_norm, wx_kv)

    n_t1 = S // ROUTE_TILE + 1
    for l in range(L):
        u, q, k, v = _mix_in(x, norm_mix, w_in_ext, q_norm, kv_norm, w_uq_ext, w_ukv, cos, sin, l, tm)
        mla = _mla_attn(q, k, v, tq)
        x, hn, aff = _mix_out(x, u, w_pool, pool_scale, mla, w_out, norm_x, wx_q, kvmem, wx_o,
                              norm_ffn, w_router_p, l, tm)
        pos, post, cnt = _select(aff, C)
        cnt = jnp.swapaxes(cnt[:, :n_t1, :N_EXPERTS], 1, 2).reshape(-1)
        post = post[:, :N_EXPERTS].reshape(B, N_EXPERTS, 1, S)
        y = _moe_ffn(cnt, post, hn, w_gate, w_up, w_down, l, C)
        x = _combine(cnt, x, y, pos, aff, norm_final, C, tt, final=(l == L - 1))
    return x
```

```python
import functools

import jax
import jax.numpy as jnp
from jax import lax
from jax.experimental import pallas as pl
from jax.experimental.pallas import tpu as pltpu

F32 = jnp.float32
BF16 = jnp.bfloat16

D_MODEL = 1024
D_POOL = 512
N_POOL_GROUPS = 4
POOL_GROUP = 128
POOL_WINDOWS = (2, 4, 8, 16)
POOL_HALO = 8
N_MLA_HEADS = 4
QK_NOPE = 128
QK_ROPE = 64
V_HEAD = 128
Q_LORA = 256
KV_LORA = 128
D_IN = D_POOL + Q_LORA + KV_LORA + QK_ROPE
ROPE_THETA = 10000.0
N_XHEADS = 4
XHEAD = D_MODEL // N_XHEADS
N_EXPERTS = 16
CAPACITY_FACTOR = 2
D_EXPERT = 2 * D_MODEL
EPS = 1e-6

LANES = 128
QK_PAD = 256
D_IN_EXT = D_POOL + Q_LORA + KV_LORA + 2 * LANES
Q_HEAD_EXT = QK_NOPE + 2 * LANES
ROUTE_TILE = 256
FFN_CHUNK = 512
V7X_VMEM_LIMIT = 56 * 1024 * 1024
NEG_BIG = -1e30
LOG2_E = 1.4426950408889634


def _cparams(*sem):
    return pltpu.CompilerParams(dimension_semantics=sem, vmem_limit_bytes=V7X_VMEM_LIMIT)


def _rms(t, gain):
    return t * lax.rsqrt(jnp.mean(t * t, axis=-1, keepdims=True) + EPS) * gain


def _dot(a, b):
    return jnp.dot(a, b, preferred_element_type=F32)


def _dot_split(a, b):
    a_hi = a.astype(BF16)
    b_hi = b.astype(BF16)
    a_lo = (a - a_hi.astype(F32)).astype(BF16)
    b_lo = (b - b_hi.astype(F32)).astype(BF16)
    return _dot(a_hi, b_hi) + (_dot(a_hi, b_lo) + _dot(a_lo, b_hi))


def _dot_nt(a, b):
    return lax.dot_general(a, b, (((1,), (1,)), ((), ())), preferred_element_type=F32)


def _memkv_kernel(mem_ref, gain_ref, w_ref, o_ref):
    h = _rms(mem_ref[0], gain_ref[0])
    o_ref[0, 0] = _dot(h.astype(BF16), w_ref[0]).astype(BF16)


def _memkv(mem, mem_norm, wx_kv):
    B, M, D = mem.shape
    L = wx_kv.shape[0]
    return pl.pallas_call(
        _memkv_kernel,
        grid=(L, B),
        in_specs=[
            pl.BlockSpec((1, M, D), lambda l, b: (b, 0, 0)),
            pl.BlockSpec((1, 1, D), lambda l, b: (l, 0, 0)),
            pl.BlockSpec((1, D, 2 * D), lambda l, b: (l, 0, 0)),
        ],
        out_specs=pl.BlockSpec((1, 1, M, 2 * D), lambda l, b: (l, b, 0, 0)),
        out_shape=jax.ShapeDtypeStruct((L, B, M, 2 * D), BF16),
        compiler_params=_cparams("parallel", "parallel"),
        name="memkv",
    )(mem, mem_norm, wx_kv)


def _mix_in_kernel(x_ref, g_ref, win_ref, qn_ref, kvn_ref, wuq_ref, wukv_ref, cos_ref, sin_ref,
                   u_ref, q_ref, k_ref, v_ref):
    h = _rms(x_ref[0], g_ref[0]).astype(BF16)
    proj = _dot(h, win_ref[0])
    u_ref[0] = proj[:, :D_POOL]
    cos = cos_ref[0]
    sin = sin_ref[0]
    o = D_POOL
    cq = _rms(proj[:, o:o + Q_LORA], qn_ref[0]).astype(BF16)
    o += Q_LORA
    ckv = _rms(proj[:, o:o + KV_LORA], kvn_ref[0]).astype(BF16)
    o += KV_LORA
    k_rope = (proj[:, o:o + LANES] * cos + proj[:, o + LANES:o + 2 * LANES] * sin).astype(BF16)
    qall = _dot(cq, wuq_ref[0])
    kvall = _dot(ckv, wukv_ref[0])
    scale = (QK_NOPE + QK_ROPE) ** -0.5 * LOG2_E
    for hd in range(N_MLA_HEADS):
        qb = hd * Q_HEAD_EXT
        q_ref[0, hd, :, :QK_NOPE] = (qall[:, qb:qb + QK_NOPE] * scale).astype(BF16)
        roped = qall[:, qb + QK_NOPE:qb + QK_NOPE + LANES] * cos + qall[:, qb + QK_NOPE + LANES:qb + Q_HEAD_EXT] * sin
        q_ref[0, hd, :, QK_NOPE:] = (roped * scale).astype(BF16)
        kb = hd * (QK_NOPE + V_HEAD)
        k_ref[0, hd, :, :QK_NOPE] = kvall[:, kb:kb + QK_NOPE].astype(BF16)
        k_ref[0, hd, :, QK_NOPE:] = k_rope
        v_ref[0, hd] = kvall[:, kb + QK_NOPE:kb + QK_NOPE + V_HEAD].astype(BF16)


def _mix_in(x, gain, w_in_ext, q_norm, kv_norm, w_uq_ext, w_ukv, cos, sin, l, tm):
    B, S, D = x.shape
    H = N_MLA_HEADS
    lmap = lambda b, i: (l, 0, 0)
    tmap = lambda b, i: (b, i, 0)
    hmap = lambda b, i: (b, 0, i, 0)
    return pl.pallas_call(
        _mix_in_kernel,
        grid=(B, S // tm),
        in_specs=[
            pl.BlockSpec((1, tm, D), tmap),
            pl.BlockSpec((1, 1, D), lmap),
            pl.BlockSpec((1, D, D_IN_EXT), lmap),
            pl.BlockSpec((1, 1, Q_LORA), lmap),
            pl.BlockSpec((1, 1, KV_LORA), lmap),
            pl.BlockSpec((1, Q_LORA, H * Q_HEAD_EXT), lmap),
            pl.BlockSpec((1, KV_LORA, H * (QK_NOPE + V_HEAD)), lmap),
            pl.BlockSpec((1, tm, LANES), tmap),
            pl.BlockSpec((1, tm, LANES), tmap),
        ],
        out_specs=[
            pl.BlockSpec((1, tm, D_POOL), tmap),
            pl.BlockSpec((1, H, tm, QK_PAD), hmap),
            pl.BlockSpec((1, H, tm, QK_PAD), hmap),
            pl.BlockSpec((1, H, tm, V_HEAD), hmap),
        ],
        out_shape=[
            jax.ShapeDtypeStruct((B, S, D_POOL), F32),
            jax.ShapeDtypeStruct((B, H, S, QK_PAD), BF16),
            jax.ShapeDtypeStruct((B, H, S, QK_PAD), BF16),
            jax.ShapeDtypeStruct((B, H, S, V_HEAD), BF16),
        ],
        compiler_params=_cparams("parallel", "parallel"),
        name="mix_in",
    )(x, gain, w_in_ext, q_norm, kv_norm, w_uq_ext, w_ukv, cos, sin)


def _mla_kernel(q_ref, k_ref, v_ref, o_ref, *, S, kc):
    q = q_ref[0, 0]
    m = l = acc = None
    for c in range(S // kc):
        rows = slice(c * kc, (c + 1) * kc)
        s = _dot_nt(q, k_ref[0, 0, rows, :])
        m_c = jnp.max(s, axis=-1, keepdims=True)
        if c == 0:
            m = m_c
            p = jnp.exp2(s - m)
            l = jnp.sum(p, axis=-1, keepdims=True)
            acc = _dot(p.astype(BF16), v_ref[0, 0, rows, :])
        else:
            m_new = jnp.maximum(m, m_c)
            alpha = jnp.exp2(m - m_new)
            p = jnp.exp2(s - m_new)
            l = alpha * l + jnp.sum(p, axis=-1, keepdims=True)
            acc = alpha * acc + _dot(p.astype(BF16), v_ref[0, 0, rows, :])
            m = m_new
    o_ref[0] = (acc / l).astype(BF16)


def _mla_attn(q, k, v, tq, kc):
    B, H, S, _ = q.shape
    return pl.pallas_call(
        functools.partial(_mla_kernel, S=S, kc=kc),
        grid=(B, H, S // tq),
        in_specs=[
            pl.BlockSpec((1, 1, tq, QK_PAD), lambda b, h, i: (b, h, i, 0)),
            pl.BlockSpec((1, 1, S, QK_PAD), lambda b, h, i: (b, h, 0, 0)),
            pl.BlockSpec((1, 1, S, V_HEAD), lambda b, h, i: (b, h, 0, 0)),
        ],
        out_specs=pl.BlockSpec((1, tq, V_HEAD), lambda b, h, i: (b, i, h)),
        out_shape=jax.ShapeDtypeStruct((B, S, H * V_HEAD), BF16),
        compiler_params=_cparams("parallel", "parallel", "parallel"),
        name="mla_attn",
    )(q, k, v)


def _mix_out_kernel(x_ref, u_ref, up_ref, un_ref, wpool_ref, ps_ref, mla_ref, wout_ref,
                    nx_ref, wxq_ref, kv_ref, wxo_ref, nf_ref, wr_ref,
                    x2_ref, hnt_ref, aff_ref, ext_ref, *, S, tm):
    i = pl.program_id(1)
    nt = pl.num_programs(1)
    hal = POOL_HALO
    ext_ref[0:hal, :] = jnp.where(i > 0, up_ref[0], 0.0)
    ext_ref[hal:hal + tm, :] = u_ref[0]
    ext_ref[hal + tm:2 * hal + tm, :] = jnp.where(i < nt - 1, un_ref[0], 0.0)
    t = i * tm + lax.broadcasted_iota(jnp.int32, (tm, POOL_GROUP), 0)
    pooled = []
    for g, w in enumerate(POOL_WINDOWS):
        left = w // 2
        right = w - 1 - left
        lanes = slice(g * POOL_GROUP, (g + 1) * POOL_GROUP)
        acc = ext_ref[pl.ds(hal - left, tm), lanes]
        for d in range(-left + 1, right + 1):
            acc = acc + ext_ref[pl.ds(hal + d, tm), lanes]
        cnt = (jnp.clip(t + right + 1, 0, S) - jnp.clip(t - left, 0, S)).astype(F32)
        diff = (acc / cnt - ext_ref[pl.ds(hal, tm), lanes]).astype(BF16)
        pooled.append(_dot(diff, wpool_ref[0, g]) * ps_ref[0, :, lanes])
    pool_out = jnp.concatenate(pooled, axis=-1).astype(BF16)
    x1 = x_ref[0] + _dot(pool_out, wout_ref[0, :D_POOL, :]) + _dot(mla_ref[0], wout_ref[0, D_POOL:, :])

    h = _rms(x1, nx_ref[0]).astype(BF16)
    q = (_dot(h, wxq_ref[0]) * (XHEAD ** -0.5)).astype(BF16)
    outs = []
    for hd in range(N_XHEADS):
        cols = slice(hd * XHEAD, (hd + 1) * XHEAD)
        kh = kv_ref[0, 0, :, cols]
        vh = kv_ref[0, 0, :, D_MODEL + hd * XHEAD:D_MODEL + (hd + 1) * XHEAD]
        s = _dot_nt(q[:, cols], kh)
        p = jnp.exp(s - jnp.max(s, axis=-1, keepdims=True))
        denom = jnp.sum(p, axis=-1, keepdims=True)
        outs.append((_dot(p.astype(BF16), vh) / denom).astype(BF16))
    x2 = x1 + _dot(jnp.concatenate(outs, axis=-1), wxo_ref[0])
    x2_ref[0] = x2

    hn = _rms(x2, nf_ref[0])
    hnt_ref[0] = hn.T.astype(BF16)
    logits = _dot_split(hn, wr_ref[0])
    lane = lax.broadcasted_iota(jnp.int32, logits.shape, 1)
    logits = jnp.where(lane < N_EXPERTS, logits, NEG_BIG)
    pe = jnp.exp(logits - jnp.max(logits, axis=-1, keepdims=True))
    aff_ref[0] = pe / jnp.sum(pe, axis=-1, keepdims=True)


def _mix_out(x, u, w_pool, pool_scale, mla, w_out, norm_x, wx_q, kvmem, wx_o, norm_ffn, w_router, l, tm):
    B, S, D = x.shape
    M = kvmem.shape[2]
    nh = tm // POOL_HALO
    last = S // POOL_HALO - 1
    lmap = lambda b, i: (l, 0, 0)
    tmap = lambda b, i: (b, i, 0)
    return pl.pallas_call(
        functools.partial(_mix_out_kernel, S=S, tm=tm),
        grid=(B, S // tm),
        in_specs=[
            pl.BlockSpec((1, tm, D), tmap),
            pl.BlockSpec((1, tm, D_POOL), tmap),
            pl.BlockSpec((1, POOL_HALO, D_POOL), lambda b, i: (b, jnp.maximum(i * nh - 1, 0), 0)),
            pl.BlockSpec((1, POOL_HALO, D_POOL), lambda b, i: (b, jnp.minimum((i + 1) * nh, last), 0)),
            pl.BlockSpec((1, N_POOL_GROUPS, POOL_GROUP, POOL_GROUP), lambda b, i: (l, 0, 0, 0)),
            pl.BlockSpec((1, 1, D_POOL), lmap),
            pl.BlockSpec((1, tm, D_POOL), tmap),
            pl.BlockSpec((1, D, D), lmap),
            pl.BlockSpec((1, 1, D), lmap),
            pl.BlockSpec((1, D, D), lmap),
            pl.BlockSpec((1, 1, M, 2 * D), lambda b, i: (l, b, 0, 0)),
            pl.BlockSpec((1, D, D), lmap),
            pl.BlockSpec((1, 1, D), lmap),
            pl.BlockSpec((1, D, LANES), lmap),
        ],
        out_specs=[
            pl.BlockSpec((1, tm, D), tmap),
            pl.BlockSpec((1, D, tm), lambda b, i: (b, 0, i)),
            pl.BlockSpec((1, tm, LANES), tmap),
        ],
        out_shape=[
            jax.ShapeDtypeStruct((B, S, D), F32),
            jax.ShapeDtypeStruct((B, D, S), BF16),
            jax.ShapeDtypeStruct((B, S, LANES), F32),
        ],
        scratch_shapes=[pltpu.VMEM((tm + 2 * POOL_HALO, D_POOL), F32)],
        compiler_params=_cparams("parallel", "parallel"),
        name="mix_out",
    )(x, u, u, u, w_pool, pool_scale, mla, w_out, norm_x, wx_q, kvmem, wx_o, norm_ffn, w_router)


def _select_kernel(aff_ref, post_ref, afft_ref, cnt_ref, *, S, C):
    rt = ROUTE_TILE
    n_tiles = S // rt
    bits = lax.bitcast_convert_type(aff_ref[0], jnp.int32)

    def search(it, prefix):
        cand = prefix | jnp.left_shift(jnp.int32(1), 30 - it)
        cnt = jnp.sum(jnp.where(bits >= cand, 1.0, 0.0), axis=0, keepdims=True)
        return jnp.where(cnt >= C, cand, prefix)

    thr = lax.fori_loop(0, 31, search, jnp.zeros((1, LANES), jnp.int32))
    n_gt = jnp.sum(jnp.where(bits > thr, 1.0, 0.0), axis=0, keepdims=True)
    need = C - n_gt
    r = lax.broadcasted_iota(jnp.int32, (rt, rt), 0)
    c = lax.broadcasted_iota(jnp.int32, (rt, rt), 1)
    tri = jnp.where(c <= r, 1.0, 0.0).astype(BF16)
    cnt_ref[...] = jnp.zeros(cnt_ref.shape, jnp.int32)
    off_eq = jnp.zeros((1, LANES), F32)
    off_sel = jnp.zeros((1, LANES), F32)
    for kt in range(n_tiles):
        rows = slice(kt * rt, (kt + 1) * rt)
        bc = bits[rows]
        gt = bc > thr
        eq = bc == thr
        eqf = jnp.where(eq, 1.0, 0.0)
        rank = off_eq + _dot(tri, eqf.astype(BF16)) - eqf
        sel = gt | (eq & (rank < need))
        self_ = jnp.where(sel, 1.0, 0.0)
        slot = off_sel + _dot(tri, self_.astype(BF16)) - self_
        post_ref[0, :, rows] = jnp.where(sel, slot, -1.0).T
        afft_ref[0, :, rows] = aff_ref[0, rows, :].T
        cnt_ref[0, kt:kt + 1, :] = off_sel.astype(jnp.int32)
        off_eq = off_eq + jnp.sum(eqf, axis=0, keepdims=True)
        off_sel = off_sel + jnp.sum(self_, axis=0, keepdims=True)
    cnt_ref[0, n_tiles:n_tiles + 1, :] = off_sel.astype(jnp.int32)


def _select(aff, C):
    B, S, _ = aff.shape
    n_rows = -(-(S // ROUTE_TILE + 1) // 8) * 8
    return pl.pallas_call(
        functools.partial(_select_kernel, S=S, C=C),
        grid=(B,),
        in_specs=[pl.BlockSpec((1, S, LANES), lambda b: (b, 0, 0))],
        out_specs=[
            pl.BlockSpec((1, LANES, S), lambda b: (b, 0, 0)),
            pl.BlockSpec((1, LANES, S), lambda b: (b, 0, 0)),
            pl.BlockSpec((1, n_rows, LANES), lambda b: (b, 0, 0)),
        ],
        out_shape=[
            jax.ShapeDtypeStruct((B, LANES, S), F32),
            jax.ShapeDtypeStruct((B, LANES, S), F32),
            jax.ShapeDtypeStruct((B, n_rows, LANES), jnp.int32),
        ],
        compiler_params=_cparams("parallel"),
        name="select",
    )(aff)


def _window(cnt_ref, base, t, C):
    w0 = jnp.minimum(cnt_ref[base + t] & -LANES, C - ROUTE_TILE)
    return pl.multiple_of(w0, LANES)


def _spills(cnt_ref, base, t, w0, jb):
    return (cnt_ref[base + t + 1] > jnp.maximum(w0 + ROUTE_TILE, jb * ROUTE_TILE)) & (w0 < jb * ROUTE_TILE)


def _onehot(prow, start, first=None):
    rel = lax.broadcasted_iota(jnp.int32, (ROUTE_TILE, ROUTE_TILE), 0).astype(F32)
    hit = (prow - lax.convert_element_type(start, F32)) == rel
    if first is not None:
        hit = hit & (rel >= lax.convert_element_type(first - start, F32))
    return jnp.where(hit, 1.0, 0.0).astype(BF16)


def _moe_ffn_kernel(cnt_ref, post_ref, hnt_ref, wg_ref, wu_ref, wd_ref, yt_ref, xgt_ref, *, S, C):
    rt = ROUTE_TILE
    n_tiles = S // rt
    e = pl.program_id(0)
    b = pl.program_id(1)
    base = (b * N_EXPERTS + e) * (n_tiles + 1)
    xgt_ref[...] = jnp.zeros(xgt_ref.shape, F32)
    starts = [_window(cnt_ref, base, t, C) for t in range(n_tiles)]
    for t in range(n_tiles):
        toks = slice(t * rt, (t + 1) * rt)
        onehot = _onehot(post_ref[0, 0, :, toks], starts[t])
        xgt_ref[:, pl.ds(starts[t], rt)] += _dot_nt(hnt_ref[0, :, toks], onehot)
    for t in range(n_tiles):
        for jb in range(1, C // rt):
            @pl.when(_spills(cnt_ref, base, t, starts[t], jb))
            def _():
                toks = slice(t * rt, (t + 1) * rt)
                onehot = _onehot(post_ref[0, 0, :, toks], jb * rt, first=starts[t] + rt)
                xgt_ref[:, jb * rt:(jb + 1) * rt] += _dot_nt(hnt_ref[0, :, toks], onehot)
    x = xgt_ref[...].T.astype(BF16)
    y = jnp.zeros((C, D_MODEL), F32)
    for c in range(D_EXPERT // FFN_CHUNK):
        cols = slice(c * FFN_CHUNK, (c + 1) * FFN_CHUNK)
        a = _dot(x, wg_ref[0, 0, :, cols])
        g = _dot(x, wu_ref[0, 0, :, cols])
        y = y + _dot((jax.nn.silu(a) * g).astype(BF16), wd_ref[0, 0, cols, :])
    yt_ref[0, 0] = y.T.astype(BF16)


def _moe_ffn(cnt, post, hnt, w_gate, w_up, w_down, l, C):
    B, D, S = hnt.shape
    E = N_EXPERTS
    return pl.pallas_call(
        functools.partial(_moe_ffn_kernel, S=S, C=C),
        grid_spec=pltpu.PrefetchScalarGridSpec(
            num_scalar_prefetch=1,
            grid=(E, B),
            in_specs=[
                pl.BlockSpec((1, 1, 1, S), lambda e, b, cnt: (b, e, 0, 0)),
                pl.BlockSpec((1, D, S), lambda e, b, cnt: (b, 0, 0)),
                pl.BlockSpec((1, 1, D, D_EXPERT), lambda e, b, cnt: (l, e, 0, 0)),
                pl.BlockSpec((1, 1, D, D_EXPERT), lambda e, b, cnt: (l, e, 0, 0)),
                pl.BlockSpec((1, 1, D_EXPERT, D), lambda e, b, cnt: (l, e, 0, 0)),
            ],
            out_specs=pl.BlockSpec((1, 1, D, C), lambda e, b, cnt: (b, e, 0, 0)),
            scratch_shapes=[pltpu.VMEM((D, C), F32)],
        ),
        out_shape=jax.ShapeDtypeStruct((B, E, D, C), BF16),
        compiler_params=_cparams("arbitrary", "arbitrary"),
        name="moe_ffn",
    )(cnt, post, hnt, w_gate, w_up, w_down)


def _combine_kernel(cnt_ref, x_ref, yt_ref, post_ref, afft_ref, gf_ref, o_ref, acc_ref, *, S, C, tt, final):
    rt = ROUTE_TILE
    n_tiles = S // rt
    b = pl.program_id(0)
    ti = pl.program_id(1)
    acc_ref[...] = jnp.zeros(acc_ref.shape, F32)
    blocks = []
    for e in range(N_EXPERTS):
        base = (b * N_EXPERTS + e) * (n_tiles + 1)
        for sub in range(tt // rt):
            blocks.append((e, base, sub, _window(cnt_ref, base, ti * (tt // rt) + sub, C)))
    for e, base, sub, w0 in blocks:
        toks = slice(sub * rt, (sub + 1) * rt)
        onehot = _onehot(post_ref[0, e:e + 1, toks], w0)
        gate = afft_ref[0, e:e + 1, toks]
        acc_ref[:, toks] += gate * _dot(yt_ref[0, e, :, pl.ds(w0, rt)], onehot)
    for e, base, sub, w0 in blocks:
        for jb in range(1, C // rt):
            @pl.when(_spills(cnt_ref, base, ti * (tt // rt) + sub, w0, jb))
            def _():
                toks = slice(sub * rt, (sub + 1) * rt)
                onehot = _onehot(post_ref[0, e:e + 1, toks], jb * rt, first=w0 + rt)
                gate = afft_ref[0, e:e + 1, toks]
                acc_ref[:, toks] += gate * _dot(yt_ref[0, e, :, jb * rt:(jb + 1) * rt], onehot)
    out = x_ref[0] + acc_ref[...].T
    if final:
        out = _rms(out, gf_ref[0])
    o_ref[0] = out


def _combine(cnt, x, yt, post, afft, norm_final, C, tt, final):
    B, S, D = x.shape
    E = N_EXPERTS
    return pl.pallas_call(
        functools.partial(_combine_kernel, S=S, C=C, tt=tt, final=final),
        grid_spec=pltpu.PrefetchScalarGridSpec(
            num_scalar_prefetch=1,
            grid=(B, S // tt),
            in_specs=[
                pl.BlockSpec((1, tt, D), lambda b, t, cnt: (b, t, 0)),
                pl.BlockSpec((1, E, D, C), lambda b, t, cnt: (b, 0, 0, 0)),
                pl.BlockSpec((1, E, tt), lambda b, t, cnt: (b, 0, t)),
                pl.BlockSpec((1, E, tt), lambda b, t, cnt: (b, 0, t)),
                pl.BlockSpec((1, 1, D), lambda b, t, cnt: (0, 0, 0)),
            ],
            out_specs=pl.BlockSpec((1, tt, D), lambda b, t, cnt: (b, t, 0)),
            scratch_shapes=[pltpu.VMEM((D, tt), F32)],
        ),
        out_shape=jax.ShapeDtypeStruct((B, S, D), F32),
        compiler_params=_cparams("parallel", "parallel"),
        name="combine",
    )(cnt, x, yt, post, afft, norm_final)


def _rot_cols(w):
    half = w.shape[-1] // 2
    return jnp.concatenate([-w[..., half:], w[..., :half]], axis=-1)


def _pad_lanes(w):
    return jnp.concatenate([w, jnp.zeros(w.shape[:-1] + (LANES - w.shape[-1],), w.dtype)], axis=-1)


def kernel(x, mem, positions, norm_mix, w_in, q_norm, kv_norm, w_uq, w_ukv, w_pool, pool_scale,
           w_out, norm_x, mem_norm, wx_q, wx_kv, wx_o, norm_ffn, w_router, w_gate, w_up, w_down,
           norm_final):
    B, S, D = x.shape
    L = w_in.shape[0]
    C = CAPACITY_FACTOR * S // N_EXPERTS
    assert D == D_MODEL and S % 1024 == 0 and C % ROUTE_TILE == 0
    tm = 512
    tq = 1024
    kc = 1024
    tt = 512

    half = QK_ROPE // 2
    inv_freq = ROPE_THETA ** (-jnp.arange(half, dtype=F32) / half)
    ang = positions.astype(F32)[..., None] * inv_freq
    cos = _pad_lanes(jnp.concatenate([jnp.cos(ang)] * 2, axis=-1))
    sin = _pad_lanes(jnp.concatenate([jnp.sin(ang)] * 2, axis=-1))

    k_r = w_in[..., D_IN - QK_ROPE:]
    w_in_ext = jnp.concatenate(
        [w_in[..., :D_IN - QK_ROPE], _pad_lanes(k_r), _pad_lanes(_rot_cols(k_r))], axis=-1).astype(BF16)
    wq = w_uq.reshape(L, Q_LORA, N_MLA_HEADS, QK_NOPE + QK_ROPE)
    q_r = wq[..., QK_NOPE:]
    w_uq_ext = jnp.concatenate([wq[..., :QK_NOPE], _pad_lanes(q_r), _pad_lanes(_rot_cols(q_r))], axis=-1)
    w_uq_ext = w_uq_ext.reshape(L, Q_LORA, N_MLA_HEADS * Q_HEAD_EXT).astype(BF16)
    w_router_p = _pad_lanes(w_router)
    row = lambda g: g.reshape(L, 1, g.shape[-1])
    norm_mix, q_norm, kv_norm, pool_scale, norm_x, mem_norm, norm_ffn = map(
        row, (norm_mix, q_norm, kv_norm, pool_scale, norm_x, mem_norm, norm_ffn))
    norm_final = norm_final.reshape(1, 1, D)
    w_ukv, w_pool, w_out, wx_q, wx_kv, wx_o, w_gate, w_up, w_down = (
        w.astype(BF16) for w in (w_ukv, w_pool, w_out, wx_q, wx_kv, wx_o, w_gate, w_up, w_down))

    kvmem = _memkv(mem, mem_norm, wx_kv)

    n_t1 = S // ROUTE_TILE + 1
    for l in range(L):
        u, q, k, v = _mix_in(x, norm_mix, w_in_ext, q_norm, kv_norm, w_uq_ext, w_ukv, cos, sin, l, tm)
        mla = _mla_attn(q, k, v, tq, kc)
        x, hnt, aff = _mix_out(x, u, w_pool, pool_scale, mla, w_out, norm_x, wx_q, kvmem, wx_o,
                               norm_ffn, w_router_p, l, tm)
        post, afft, cnt = _select(aff, C)
        cnt = jnp.swapaxes(cnt[:, :n_t1, :N_EXPERTS], 1, 2).reshape(-1)
        post_e = post[:, :N_EXPERTS].reshape(B, N_EXPERTS, 1, S)
        yt = _moe_ffn(cnt, post_e, hnt, w_gate, w_up, w_down, l, C)
        x = _combine(cnt, x, yt, post, afft, norm_final, C, tt, final=(l == L - 1))
    return x
```

```python
import functools

import jax
import jax.numpy as jnp
from jax import lax
from jax.experimental import pallas as pl
from jax.experimental.pallas import tpu as pltpu

F32 = jnp.float32
BF16 = jnp.bfloat16

D_MODEL = 1024
D_POOL = 512
N_POOL_GROUPS = 4
POOL_GROUP = 128
POOL_WINDOWS = (2, 4, 8, 16)
POOL_HALO = 8
N_MLA_HEADS = 4
QK_NOPE = 128
QK_ROPE = 64
V_HEAD = 128
Q_LORA = 256
KV_LORA = 128
D_IN = D_POOL + Q_LORA + KV_LORA + QK_ROPE
ROPE_THETA = 10000.0
N_XHEADS = 4
XHEAD = D_MODEL // N_XHEADS
N_EXPERTS = 16
CAPACITY_FACTOR = 2
D_EXPERT = 2 * D_MODEL
EPS = 1e-6

LANES = 128
QK_PAD = 256
D_IN_EXT = D_POOL + Q_LORA + KV_LORA + 2 * LANES
Q_HEAD_EXT = QK_NOPE + 2 * LANES
ROUTE_TILE = 256
FFN_CHUNK = 512
V7X_VMEM_LIMIT = 56 * 1024 * 1024
LOG2_E = 1.4426950408889634


def _cparams(*sem):
    return pltpu.CompilerParams(dimension_semantics=sem, vmem_limit_bytes=V7X_VMEM_LIMIT)


def _rms(t, gain):
    return t * lax.rsqrt(jnp.mean(t * t, axis=-1, keepdims=True) + EPS) * gain


def _dot(a, b):
    return jnp.dot(a, b, preferred_element_type=F32)


def _dot_split(a, b):
    a_hi = a.astype(BF16)
    b_hi = b.astype(BF16)
    a_lo = (a - a_hi.astype(F32)).astype(BF16)
    b_lo = (b - b_hi.astype(F32)).astype(BF16)
    return _dot(a_hi, b_hi) + (_dot(a_hi, b_lo) + _dot(a_lo, b_hi))


def _dot_nt(a, b):
    return lax.dot_general(a, b, (((1,), (1,)), ((), ())), preferred_element_type=F32)


def _memkv_kernel(mem_ref, gain_ref, w_ref, o_ref):
    h = _rms(mem_ref[0], gain_ref[0])
    o_ref[0, 0] = _dot(h.astype(BF16), w_ref[0]).astype(BF16)


def _memkv(mem, mem_norm, wx_kv):
    B, M, D = mem.shape
    L = wx_kv.shape[0]
    return pl.pallas_call(
        _memkv_kernel,
        grid=(L, B),
        in_specs=[
            pl.BlockSpec((1, M, D), lambda l, b: (b, 0, 0)),
            pl.BlockSpec((1, 1, D), lambda l, b: (l, 0, 0)),
            pl.BlockSpec((1, D, 2 * D), lambda l, b: (l, 0, 0)),
        ],
        out_specs=pl.BlockSpec((1, 1, M, 2 * D), lambda l, b: (l, b, 0, 0)),
        out_shape=jax.ShapeDtypeStruct((L, B, M, 2 * D), BF16),
        compiler_params=_cparams("parallel", "parallel"),
        name="memkv",
    )(mem, mem_norm, wx_kv)


def _mix_in_kernel(x_ref, g_ref, win_ref, qn_ref, kvn_ref, wuq_ref, wukv_ref, cos_ref, sin_ref,
                   u_ref, q_ref, k_ref, v_ref):
    h = _rms(x_ref[0], g_ref[0]).astype(BF16)
    proj = _dot(h, win_ref[0])
    u_ref[0] = proj[:, :D_POOL]
    cos = cos_ref[0]
    sin = sin_ref[0]
    o = D_POOL
    cq = _rms(proj[:, o:o + Q_LORA], qn_ref[0]).astype(BF16)
    o += Q_LORA
    ckv = _rms(proj[:, o:o + KV_LORA], kvn_ref[0]).astype(BF16)
    o += KV_LORA
    k_rope = (proj[:, o:o + LANES] * cos + proj[:, o + LANES:o + 2 * LANES] * sin).astype(BF16)
    qall = _dot(cq, wuq_ref[0])
    kvall = _dot(ckv, wukv_ref[0])
    scale = (QK_NOPE + QK_ROPE) ** -0.5 * LOG2_E
    for hd in range(N_MLA_HEADS):
        qb = hd * Q_HEAD_EXT
        q_ref[0, hd, :, :QK_NOPE] = (qall[:, qb:qb + QK_NOPE] * scale).astype(BF16)
        roped = qall[:, qb + QK_NOPE:qb + QK_NOPE + LANES] * cos + qall[:, qb + QK_NOPE + LANES:qb + Q_HEAD_EXT] * sin
        q_ref[0, hd, :, QK_NOPE:] = (roped * scale).astype(BF16)
        kb = hd * (QK_NOPE + V_HEAD)
        k_ref[0, hd, :, :QK_NOPE] = kvall[:, kb:kb + QK_NOPE].astype(BF16)
        k_ref[0, hd, :, QK_NOPE:] = k_rope
        v_ref[0, hd] = kvall[:, kb + QK_NOPE:kb + QK_NOPE + V_HEAD].astype(BF16)


def _mix_in(x, gain, w_in_ext, q_norm, kv_norm, w_uq_ext, w_ukv, cos, sin, l, tm):
    B, S, D = x.shape
    H = N_MLA_HEADS
    lmap = lambda b, i: (l, 0, 0)
    tmap = lambda b, i: (b, i, 0)
    hmap = lambda b, i: (b, 0, i, 0)
    return pl.pallas_call(
        _mix_in_kernel,
        grid=(B, S // tm),
        in_specs=[
            pl.BlockSpec((1, tm, D), tmap),
            pl.BlockSpec((1, 1, D), lmap),
            pl.BlockSpec((1, D, D_IN_EXT), lmap),
            pl.BlockSpec((1, 1, Q_LORA), lmap),
            pl.BlockSpec((1, 1, KV_LORA), lmap),
            pl.BlockSpec((1, Q_LORA, H * Q_HEAD_EXT), lmap),
            pl.BlockSpec((1, KV_LORA, H * (QK_NOPE + V_HEAD)), lmap),
            pl.BlockSpec((1, tm, LANES), tmap),
            pl.BlockSpec((1, tm, LANES), tmap),
        ],
        out_specs=[
            pl.BlockSpec((1, tm, D_POOL), tmap),
            pl.BlockSpec((1, H, tm, QK_PAD), hmap),
            pl.BlockSpec((1, H, tm, QK_PAD), hmap),
            pl.BlockSpec((1, H, tm, V_HEAD), hmap),
        ],
        out_shape=[
            jax.ShapeDtypeStruct((B, S, D_POOL), F32),
            jax.ShapeDtypeStruct((B, H, S, QK_PAD), BF16),
            jax.ShapeDtypeStruct((B, H, S, QK_PAD), BF16),
            jax.ShapeDtypeStruct((B, H, S, V_HEAD), BF16),
        ],
        compiler_params=_cparams("parallel", "parallel"),
        name="mix_in",
    )(x, gain, w_in_ext, q_norm, kv_norm, w_uq_ext, w_ukv, cos, sin)


def _mla_kernel(q_ref, k_ref, v_ref, o_ref, *, S, kc):
    q = q_ref[0, 0]
    m = l = acc = None
    for c in range(S // kc):
        rows = slice(c * kc, (c + 1) * kc)
        s = _dot_nt(q, k_ref[0, 0, rows, :])
        m_c = jnp.max(s, axis=-1, keepdims=True)
        if c == 0:
            m = m_c
            p = jnp.exp2(s - m)
            l = jnp.sum(p, axis=-1, keepdims=True)
            acc = _dot(p.astype(BF16), v_ref[0, 0, rows, :])
        else:
            m_new = jnp.maximum(m, m_c)
            alpha = jnp.exp2(m - m_new)
            p = jnp.exp2(s - m_new)
            l = alpha * l + jnp.sum(p, axis=-1, keepdims=True)
            acc = alpha * acc + _dot(p.astype(BF16), v_ref[0, 0, rows, :])
            m = m_new
    o_ref[0] = (acc / l).astype(BF16)


def _mla_attn(q, k, v, tq, kc):
    B, H, S, _ = q.shape
    return pl.pallas_call(
        functools.partial(_mla_kernel, S=S, kc=kc),
        grid=(B, H, S // tq),
        in_specs=[
            pl.BlockSpec((1, 1, tq, QK_PAD), lambda b, h, i: (b, h, i, 0)),
            pl.BlockSpec((1, 1, S, QK_PAD), lambda b, h, i: (b, h, 0, 0)),
            pl.BlockSpec((1, 1, S, V_HEAD), lambda b, h, i: (b, h, 0, 0)),
        ],
        out_specs=pl.BlockSpec((1, tq, V_HEAD), lambda b, h, i: (b, i, h)),
        out_shape=jax.ShapeDtypeStruct((B, S, H * V_HEAD), BF16),
        compiler_params=_cparams("parallel", "parallel", "parallel"),
        name="mla_attn",
    )(q, k, v)


def _mix_out_kernel(x_ref, u_ref, up_ref, un_ref, wpool_ref, ps_ref, mla_ref, wout_ref,
                    nx_ref, wxq_ref, kv_ref, wxo_ref, nf_ref, wr_ref,
                    x2_ref, hnt_ref, afft_ref, ext_ref, *, S, tm, rs):
    i = pl.program_id(1)
    nt = pl.num_programs(1)
    hal = POOL_HALO
    ext_ref[0:hal, :] = jnp.where(i > 0, up_ref[0], 0.0)
    ext_ref[hal:hal + tm, :] = u_ref[0]
    ext_ref[hal + tm:2 * hal + tm, :] = jnp.where(i < nt - 1, un_ref[0], 0.0)
    for r0 in range(0, tm, rs):
        rows = slice(r0, r0 + rs)
        t = i * tm + r0 + lax.broadcasted_iota(jnp.int32, (rs, POOL_GROUP), 0)
        pooled = []
        for g, w in enumerate(POOL_WINDOWS):
            left = w // 2
            right = w - 1 - left
            lanes = slice(g * POOL_GROUP, (g + 1) * POOL_GROUP)
            acc = ext_ref[pl.ds(r0 + hal - left, rs), lanes]
            for d in range(-left + 1, right + 1):
                acc = acc + ext_ref[pl.ds(r0 + hal + d, rs), lanes]
            cnt = (jnp.clip(t + right + 1, 0, S) - jnp.clip(t - left, 0, S)).astype(F32)
            diff = (acc / cnt - ext_ref[pl.ds(r0 + hal, rs), lanes]).astype(BF16)
            pooled.append(_dot(diff, wpool_ref[0, g]) * ps_ref[0, :, lanes])
        pool_out = jnp.concatenate(pooled, axis=-1).astype(BF16)
        x1 = (x_ref[0, rows, :] + _dot(pool_out, wout_ref[0, :D_POOL, :])
              + _dot(mla_ref[0, rows, :], wout_ref[0, D_POOL:, :]))

        h = _rms(x1, nx_ref[0]).astype(BF16)
        q = (_dot(h, wxq_ref[0]) * (XHEAD ** -0.5)).astype(BF16)
        outs = []
        for hd in range(N_XHEADS):
            cols = slice(hd * XHEAD, (hd + 1) * XHEAD)
            kh = kv_ref[0, 0, :, cols]
            vh = kv_ref[0, 0, :, D_MODEL + hd * XHEAD:D_MODEL + (hd + 1) * XHEAD]
            s = _dot_nt(q[:, cols], kh)
            p = jnp.exp(s - jnp.max(s, axis=-1, keepdims=True))
            denom = jnp.sum(p, axis=-1, keepdims=True)
            outs.append((_dot(p.astype(BF16), vh) / denom).astype(BF16))
        x2 = x1 + _dot(jnp.concatenate(outs, axis=-1), wxo_ref[0])
        x2_ref[0, rows, :] = x2

        hn = _rms(x2, nf_ref[0])
        hnt_ref[0, :, rows] = hn.T.astype(BF16)
        logits = _dot_split(hn, wr_ref[0]).T[:N_EXPERTS]
        pe = jnp.exp(logits - jnp.max(logits, axis=0, keepdims=True))
        afft_ref[0, :, rows] = pe / jnp.sum(pe, axis=0, keepdims=True)


def _mix_out(x, u, w_pool, pool_scale, mla, w_out, norm_x, wx_q, kvmem, wx_o, norm_ffn, w_router, l, tm, rs):
    B, S, D = x.shape
    M = kvmem.shape[2]
    nh = tm // POOL_HALO
    last = S // POOL_HALO - 1
    lmap = lambda b, i: (l, 0, 0)
    tmap = lambda b, i: (b, i, 0)
    return pl.pallas_call(
        functools.partial(_mix_out_kernel, S=S, tm=tm, rs=rs),
        grid=(B, S // tm),
        in_specs=[
            pl.BlockSpec((1, tm, D), tmap),
            pl.BlockSpec((1, tm, D_POOL), tmap),
            pl.BlockSpec((1, POOL_HALO, D_POOL), lambda b, i: (b, jnp.maximum(i * nh - 1, 0), 0)),
            pl.BlockSpec((1, POOL_HALO, D_POOL), lambda b, i: (b, jnp.minimum((i + 1) * nh, last), 0)),
            pl.BlockSpec((1, N_POOL_GROUPS, POOL_GROUP, POOL_GROUP), lambda b, i: (l, 0, 0, 0)),
            pl.BlockSpec((1, 1, D_POOL), lmap),
            pl.BlockSpec((1, tm, D_POOL), tmap),
            pl.BlockSpec((1, D, D), lmap),
            pl.BlockSpec((1, 1, D), lmap),
            pl.BlockSpec((1, D, D), lmap),
            pl.BlockSpec((1, 1, M, 2 * D), lambda b, i: (l, b, 0, 0)),
            pl.BlockSpec((1, D, D), lmap),
            pl.BlockSpec((1, 1, D), lmap),
            pl.BlockSpec((1, D, LANES), lmap),
        ],
        out_specs=[
            pl.BlockSpec((1, tm, D), tmap),
            pl.BlockSpec((1, D, tm), lambda b, i: (b, 0, i)),
            pl.BlockSpec((1, N_EXPERTS, tm), lambda b, i: (b, 0, i)),
        ],
        out_shape=[
            jax.ShapeDtypeStruct((B, S, D), F32),
            jax.ShapeDtypeStruct((B, D, S), BF16),
            jax.ShapeDtypeStruct((B, N_EXPERTS, S), F32),
        ],
        scratch_shapes=[pltpu.VMEM((tm + 2 * POOL_HALO, D_POOL), F32)],
        compiler_params=_cparams("parallel", "parallel"),
        name="mix_out",
    )(x, u, u, u, w_pool, pool_scale, mla, w_out, norm_x, wx_q, kvmem, wx_o, norm_ffn, w_router)


def _select_kernel(afft_ref, post_ref, cnt_ref, *, S, C):
    rt = ROUTE_TILE
    n_tiles = S // rt
    E = N_EXPERTS
    bits = lax.bitcast_convert_type(afft_ref[0], jnp.int32)

    def search(it, prefix):
        cand = prefix | jnp.left_shift(jnp.int32(1), 30 - it)
        cnt = jnp.sum(jnp.where(bits >= cand, 1.0, 0.0), axis=1, keepdims=True)
        return jnp.where(cnt >= C, cand, prefix)

    thr = lax.fori_loop(0, 31, search, jnp.zeros((E, 1), jnp.int32))
    n_gt = jnp.sum(jnp.where(bits > thr, 1.0, 0.0), axis=1, keepdims=True)
    need = C - n_gt
    r = lax.broadcasted_iota(jnp.int32, (rt, rt), 0)
    c = lax.broadcasted_iota(jnp.int32, (rt, rt), 1)
    tri = jnp.where(r <= c, 1.0, 0.0).astype(BF16)
    lane = lax.broadcasted_iota(jnp.int32, (E, LANES), 1)
    counts = jnp.zeros((E, LANES), F32)
    off_eq = jnp.zeros((E, 1), F32)
    off_sel = jnp.zeros((E, 1), F32)
    for kt in range(n_tiles):
        toks = slice(kt * rt, (kt + 1) * rt)
        bc = bits[:, toks]
        gt = bc > thr
        eq = bc == thr
        eqf = jnp.where(eq, 1.0, 0.0)
        rank = off_eq + _dot(eqf.astype(BF16), tri) - eqf
        sel = gt | (eq & (rank < need))
        self_ = jnp.where(sel, 1.0, 0.0)
        slot = off_sel + _dot(self_.astype(BF16), tri) - self_
        post_ref[0, :, toks] = jnp.where(sel, slot, -1.0)
        counts = jnp.where(lane == kt, off_sel, counts)
        off_eq = off_eq + jnp.sum(eqf, axis=1, keepdims=True)
        off_sel = off_sel + jnp.sum(self_, axis=1, keepdims=True)
    cnt_ref[0] = jnp.where(lane == n_tiles, off_sel, counts).astype(jnp.int32)


def _select(afft, C):
    B, E, S = afft.shape
    assert S // ROUTE_TILE < LANES
    return pl.pallas_call(
        functools.partial(_select_kernel, S=S, C=C),
        grid=(B,),
        in_specs=[pl.BlockSpec((1, E, S), lambda b: (b, 0, 0))],
        out_specs=[
            pl.BlockSpec((1, E, S), lambda b: (b, 0, 0)),
            pl.BlockSpec((1, E, LANES), lambda b: (b, 0, 0)),
        ],
        out_shape=[
            jax.ShapeDtypeStruct((B, E, S), F32),
            jax.ShapeDtypeStruct((B, E, LANES), jnp.int32),
        ],
        compiler_params=_cparams("parallel"),
        name="select",
    )(afft)


def _window(first_slot, C):
    return pl.multiple_of(jnp.minimum(first_slot & -LANES, C - ROUTE_TILE), LANES)


def _spills(end_slot, w0, jb):
    return (end_slot > jnp.maximum(w0 + ROUTE_TILE, jb * ROUTE_TILE)) & (w0 < jb * ROUTE_TILE)


def _onehot(prow, start, first=None):
    rel = lax.broadcasted_iota(jnp.int32, (ROUTE_TILE, prow.shape[1]), 0).astype(F32)
    hit = (prow - lax.convert_element_type(start, F32)) == rel
    if first is not None:
        hit = hit & (rel >= lax.convert_element_type(first - start, F32))
    return jnp.where(hit, 1.0, 0.0).astype(BF16)


def _moe_ffn_kernel(cnt_ref, post_ref, hnt_ref, wg_ref, wu_ref, wd_ref, yt_ref,
                    xgt_ref, wg_s, wu_s, wd_s, *, S, C, span):
    rt = ROUTE_TILE
    g = pl.program_id(0)
    b = pl.program_id(1)

    @pl.when(g < N_EXPERTS)
    def _():
        nxt = g & 1
        pr, pd = wg_ref.shape[2], wd_ref.shape[2]
        r0 = pl.multiple_of(b * pr, pr)
        d0 = pl.multiple_of(b * pd, pd)
        wg_s[nxt, pl.ds(r0, pr), :] = wg_ref[0, 0].astype(BF16)
        wu_s[nxt, pl.ds(r0, pr), :] = wu_ref[0, 0].astype(BF16)
        wd_s[nxt, pl.ds(d0, pd), :] = wd_ref[0, 0].astype(BF16)

    @pl.when(g == 0)
    def _():
        yt_ref[0, 0] = jnp.zeros(yt_ref.shape[2:], BF16)

    @pl.when(g > 0)
    def _():
        e = g - 1
        cur = e & 1
        base = (b * N_EXPERTS + e) * (S // rt + 1)
        xgt_ref[...] = jnp.zeros(xgt_ref.shape, BF16)
        n_spans = S // span
        bounds = [cnt_ref[base + t * (span // rt)] for t in range(n_spans + 1)]
        starts = [_window(bounds[t], C) for t in range(n_spans)]
        for t in range(n_spans):
            toks = slice(t * span, (t + 1) * span)
            onehot = _onehot(post_ref[0, 0, :, toks], starts[t])
            xgt_ref[:, pl.ds(starts[t], rt)] += _dot_nt(hnt_ref[0, :, toks], onehot).astype(BF16)
        for t in range(n_spans):
            for jb in range(1, C // rt):
                @pl.when(_spills(bounds[t + 1], starts[t], jb))
                def _():
                    toks = slice(t * span, (t + 1) * span)
                    onehot = _onehot(post_ref[0, 0, :, toks], jb * rt, first=starts[t] + rt)
                    xgt_ref[:, jb * rt:(jb + 1) * rt] += _dot_nt(hnt_ref[0, :, toks], onehot).astype(BF16)
        x = xgt_ref[...].T
        y = jnp.zeros((C, D_MODEL), F32)
        for c in range(D_EXPERT // FFN_CHUNK):
            cols = slice(c * FFN_CHUNK, (c + 1) * FFN_CHUNK)
            a = _dot(x, wg_s[cur, :, cols])
            gt = _dot(x, wu_s[cur, :, cols])
            y = y + _dot((jax.nn.silu(a) * gt).astype(BF16), wd_s[cur, cols, :])
        yt_ref[0, 0] = y.T.astype(BF16)


def _moe_ffn(cnt, post, hnt, w_gate, w_up, w_down, l, C, span):
    B, D, S = hnt.shape
    E = N_EXPERTS
    F = D_EXPERT
    assert D % B == 0 and F % B == 0 and (D // B) % 16 == 0
    wmap = lambda g, b, cnt: (l, jnp.minimum(g, E - 1), b, 0)
    emap = lambda g, b, cnt: (b, jnp.maximum(g - 1, 0), 0, 0)
    return pl.pallas_call(
        functools.partial(_moe_ffn_kernel, S=S, C=C, span=span),
        grid_spec=pltpu.PrefetchScalarGridSpec(
            num_scalar_prefetch=1,
            grid=(E + 1, B),
            in_specs=[
                pl.BlockSpec((1, 1, 1, S), emap),
                pl.BlockSpec((1, D, S), lambda g, b, cnt: (b, 0, 0)),
                pl.BlockSpec((1, 1, D // B, F), wmap),
                pl.BlockSpec((1, 1, D // B, F), wmap),
                pl.BlockSpec((1, 1, F // B, D), wmap),
            ],
            out_specs=pl.BlockSpec((1, 1, D, C), lambda g, b, cnt: (b, jnp.where(g == 0, E, g - 1), 0, 0)),
            scratch_shapes=[
                pltpu.VMEM((D, C), BF16),
                pltpu.VMEM((2, D, F), BF16),
                pltpu.VMEM((2, D, F), BF16),
                pltpu.VMEM((2, F, D), BF16),
            ],
        ),
        out_shape=jax.ShapeDtypeStruct((B, E + 1, D, C), BF16),
        compiler_params=_cparams("arbitrary", "arbitrary"),
        name="moe_ffn",
    )(cnt, post, hnt, w_gate, w_up, w_down)


def _combine_kernel(cnt_ref, x_ref, yt_ref, post_ref, afft_ref, gf_ref, o_ref, acc_ref, *, S, C, tt, final):
    rt = ROUTE_TILE
    n_tiles = S // rt
    b = pl.program_id(0)
    ti = pl.program_id(1)
    acc_ref[...] = jnp.zeros(acc_ref.shape, F32)
    blocks = []
    for e in range(N_EXPERTS):
        base = (b * N_EXPERTS + e) * (n_tiles + 1)
        for sub in range(tt // rt):
            tile = ti * (tt // rt) + sub
            blocks.append((e, sub, cnt_ref[base + tile + 1], _window(cnt_ref[base + tile], C)))
    for e, sub, end, w0 in blocks:
        toks = slice(sub * rt, (sub + 1) * rt)
        onehot = _onehot(post_ref[0, e:e + 1, toks], w0)
        gate = afft_ref[0, e:e + 1, toks]
        acc_ref[:, toks] += gate * _dot(yt_ref[0, e, :, pl.ds(w0, rt)], onehot)
    for e, sub, end, w0 in blocks:
        for jb in range(1, C // rt):
            @pl.when(_spills(end, w0, jb))
            def _():
                toks = slice(sub * rt, (sub + 1) * rt)
                onehot = _onehot(post_ref[0, e:e + 1, toks], jb * rt, first=w0 + rt)
                gate = afft_ref[0, e:e + 1, toks]
                acc_ref[:, toks] += gate * _dot(yt_ref[0, e, :, jb * rt:(jb + 1) * rt], onehot)
    out = x_ref[0] + acc_ref[...].T
    if final:
        out = _rms(out, gf_ref[0])
    o_ref[0] = out


def _combine(cnt, x, yt, post, afft, norm_final, C, tt, final):
    B, S, D = x.shape
    E = N_EXPERTS
    return pl.pallas_call(
        functools.partial(_combine_kernel, S=S, C=C, tt=tt, final=final),
        grid_spec=pltpu.PrefetchScalarGridSpec(
            num_scalar_prefetch=1,
            grid=(B, S // tt),
            in_specs=[
                pl.BlockSpec((1, tt, D), lambda b, t, cnt: (b, t, 0)),
                pl.BlockSpec((1, E, D, C), lambda b, t, cnt: (b, 0, 0, 0)),
                pl.BlockSpec((1, E, tt), lambda b, t, cnt: (b, 0, t)),
                pl.BlockSpec((1, E, tt), lambda b, t, cnt: (b, 0, t)),
                pl.BlockSpec((1, 1, D), lambda b, t, cnt: (0, 0, 0)),
            ],
            out_specs=pl.BlockSpec((1, tt, D), lambda b, t, cnt: (b, t, 0)),
            scratch_shapes=[pltpu.VMEM((D, tt), F32)],
        ),
        out_shape=jax.ShapeDtypeStruct((B, S, D), F32),
        compiler_params=_cparams("parallel", "parallel"),
        name="combine",
    )(cnt, x, yt, post, afft, norm_final)


def _rot_cols(w):
    half = w.shape[-1] // 2
    return jnp.concatenate([-w[..., half:], w[..., :half]], axis=-1)


def _pad_lanes(w):
    return jnp.concatenate([w, jnp.zeros(w.shape[:-1] + (LANES - w.shape[-1],), w.dtype)], axis=-1)


def kernel(x, mem, positions, norm_mix, w_in, q_norm, kv_norm, w_uq, w_ukv, w_pool, pool_scale,
           w_out, norm_x, mem_norm, wx_q, wx_kv, wx_o, norm_ffn, w_router, w_gate, w_up, w_down,
           norm_final):
    B, S, D = x.shape
    L = w_in.shape[0]
    C = CAPACITY_FACTOR * S // N_EXPERTS
    assert D == D_MODEL and S % 1024 == 0 and C % ROUTE_TILE == 0
    tm = 512
    tmo = 512
    rs = 256
    tq = 1024
    kc = 1024
    tt = 512
    span = 512

    half = QK_ROPE // 2
    inv_freq = ROPE_THETA ** (-jnp.arange(half, dtype=F32) / half)
    ang = positions.astype(F32)[..., None] * inv_freq
    cos = _pad_lanes(jnp.concatenate([jnp.cos(ang)] * 2, axis=-1))
    sin = _pad_lanes(jnp.concatenate([jnp.sin(ang)] * 2, axis=-1))

    k_r = w_in[..., D_IN - QK_ROPE:]
    w_in_ext = jnp.concatenate(
        [w_in[..., :D_IN - QK_ROPE], _pad_lanes(k_r), _pad_lanes(_rot_cols(k_r))], axis=-1).astype(BF16)
    wq = w_uq.reshape(L, Q_LORA, N_MLA_HEADS, QK_NOPE + QK_ROPE)
    q_r = wq[..., QK_NOPE:]
    w_uq_ext = jnp.concatenate([wq[..., :QK_NOPE], _pad_lanes(q_r), _pad_lanes(_rot_cols(q_r))], axis=-1)
    w_uq_ext = w_uq_ext.reshape(L, Q_LORA, N_MLA_HEADS * Q_HEAD_EXT).astype(BF16)
    w_router_p = _pad_lanes(w_router)
    row = lambda g: g.reshape(L, 1, g.shape[-1])
    norm_mix, q_norm, kv_norm, pool_scale, norm_x, mem_norm, norm_ffn = map(
        row, (norm_mix, q_norm, kv_norm, pool_scale, norm_x, mem_norm, norm_ffn))
    norm_final = norm_final.reshape(1, 1, D)
    w_ukv, w_pool, w_out, wx_q, wx_kv, wx_o = (
        w.astype(BF16) for w in (w_ukv, w_pool, w_out, wx_q, wx_kv, wx_o))

    kvmem = _memkv(mem, mem_norm, wx_kv)

    n_t1 = S // ROUTE_TILE + 1
    for l in range(L):
        u, q, k, v = _mix_in(x, norm_mix, w_in_ext, q_norm, kv_norm, w_uq_ext, w_ukv, cos, sin, l, tm)
        mla = _mla_attn(q, k, v, tq, kc)
        x, hnt, afft = _mix_out(x, u, w_pool, pool_scale, mla, w_out, norm_x, wx_q, kvmem, wx_o,
                                norm_ffn, w_router_p, l, tmo, rs)
        post, cnt = _select(afft, C)
        cnt = cnt[:, :, :n_t1].reshape(-1)
        post_e = post.reshape(B, N_EXPERTS, 1, S)
        yt = _moe_ffn(cnt, post_e, hnt, w_gate, w_up, w_down, l, C, span)
        x = _combine(cnt, x, yt, post, afft, norm_final, C, tt, final=(l == L - 1))
    return x
```

```python
import functools

import jax
import jax.numpy as jnp
from jax import lax
from jax.experimental import pallas as pl
from jax.experimental.pallas import tpu as pltpu

F32 = jnp.float32
BF16 = jnp.bfloat16

D_MODEL = 1024
D_POOL = 512
N_POOL_GROUPS = 4
POOL_GROUP = 128
POOL_WINDOWS = (2, 4, 8, 16)
POOL_HALO = 8
N_MLA_HEADS = 4
QK_NOPE = 128
QK_ROPE = 64
V_HEAD = 128
Q_LORA = 256
KV_LORA = 128
D_IN = D_POOL + Q_LORA + KV_LORA + QK_ROPE
ROPE_THETA = 10000.0
N_XHEADS = 4
XHEAD = D_MODEL // N_XHEADS
N_EXPERTS = 16
CAPACITY_FACTOR = 2
D_EXPERT = 2 * D_MODEL
EPS = 1e-6

LANES = 128
QK_PAD = 256
D_IN_EXT = D_POOL + Q_LORA + KV_LORA + 2 * LANES
Q_HEAD_EXT = QK_NOPE + 2 * LANES
ROUTE_TILE = 256
FFN_CHUNK = 512
V7X_VMEM_LIMIT = 56 * 1024 * 1024
LOG2_E = 1.4426950408889634


def _cparams(*sem):
    return pltpu.CompilerParams(dimension_semantics=sem, vmem_limit_bytes=V7X_VMEM_LIMIT)


def _rms(t, gain):
    return t * lax.rsqrt(jnp.mean(t * t, axis=-1, keepdims=True) + EPS) * gain


def _dot(a, b):
    return jnp.dot(a, b, preferred_element_type=F32)


def _dot_nt(a, b):
    return lax.dot_general(a, b, (((1,), (1,)), ((), ())), preferred_element_type=F32)


def _memkv_kernel(mem_ref, gain_ref, w_ref, o_ref):
    h = _rms(mem_ref[0], gain_ref[0])
    o_ref[0, 0] = _dot(h.astype(BF16), w_ref[0]).astype(BF16)


def _memkv(mem, mem_norm, wx_kv):
    B, M, D = mem.shape
    L = wx_kv.shape[0]
    return pl.pallas_call(
        _memkv_kernel,
        grid=(L, B),
        in_specs=[
            pl.BlockSpec((1, M, D), lambda l, b: (b, 0, 0)),
            pl.BlockSpec((1, 1, D), lambda l, b: (l, 0, 0)),
            pl.BlockSpec((1, D, 2 * D), lambda l, b: (l, 0, 0)),
        ],
        out_specs=pl.BlockSpec((1, 1, M, 2 * D), lambda l, b: (l, b, 0, 0)),
        out_shape=jax.ShapeDtypeStruct((L, B, M, 2 * D), BF16),
        compiler_params=_cparams("parallel", "parallel"),
        name="memkv",
    )(mem, mem_norm, wx_kv)


def _mix_in_kernel(x_ref, g_ref, win_ref, qn_ref, kvn_ref, wuq_ref, wukv_ref, cos_ref, sin_ref,
                   u_ref, q_ref, k_ref, v_ref):
    h = _rms(x_ref[0], g_ref[0]).astype(BF16)
    proj = _dot(h, win_ref[0])
    u_ref[0] = proj[:, :D_POOL]
    cos = cos_ref[0]
    sin = sin_ref[0]
    o = D_POOL
    cq = _rms(proj[:, o:o + Q_LORA], qn_ref[0]).astype(BF16)
    o += Q_LORA
    ckv = _rms(proj[:, o:o + KV_LORA], kvn_ref[0]).astype(BF16)
    o += KV_LORA
    k_rope = (proj[:, o:o + LANES] * cos + proj[:, o + LANES:o + 2 * LANES] * sin).astype(BF16)
    qall = _dot(cq, wuq_ref[0])
    kvall = _dot(ckv, wukv_ref[0])
    scale = (QK_NOPE + QK_ROPE) ** -0.5 * LOG2_E
    for hd in range(N_MLA_HEADS):
        qb = hd * Q_HEAD_EXT
        q_ref[0, hd, :, :QK_NOPE] = (qall[:, qb:qb + QK_NOPE] * scale).astype(BF16)
        roped = qall[:, qb + QK_NOPE:qb + QK_NOPE + LANES] * cos + qall[:, qb + QK_NOPE + LANES:qb + Q_HEAD_EXT] * sin
        q_ref[0, hd, :, QK_NOPE:] = (roped * scale).astype(BF16)
        kb = hd * (QK_NOPE + V_HEAD)
        k_ref[0, hd, :, :QK_NOPE] = kvall[:, kb:kb + QK_NOPE].astype(BF16)
        k_ref[0, hd, :, QK_NOPE:] = k_rope
        v_ref[0, hd] = kvall[:, kb + QK_NOPE:kb + QK_NOPE + V_HEAD].astype(BF16)


def _mix_in(x, gain, w_in_ext, q_norm, kv_norm, w_uq_ext, w_ukv, cos, sin, l, tm):
    B, S, D = x.shape
    H = N_MLA_HEADS
    lmap = lambda b, i: (l, 0, 0)
    tmap = lambda b, i: (b, i, 0)
    hmap = lambda b, i: (b, 0, i, 0)
    return pl.pallas_call(
        _mix_in_kernel,
        grid=(B, S // tm),
        in_specs=[
            pl.BlockSpec((1, tm, D), tmap),
            pl.BlockSpec((1, 1, D), lmap),
            pl.BlockSpec((1, D, D_IN_EXT), lmap),
            pl.BlockSpec((1, 1, Q_LORA), lmap),
            pl.BlockSpec((1, 1, KV_LORA), lmap),
            pl.BlockSpec((1, Q_LORA, H * Q_HEAD_EXT), lmap),
            pl.BlockSpec((1, KV_LORA, H * (QK_NOPE + V_HEAD)), lmap),
            pl.BlockSpec((1, tm, LANES), tmap),
            pl.BlockSpec((1, tm, LANES), tmap),
        ],
        out_specs=[
            pl.BlockSpec((1, tm, D_POOL), tmap),
            pl.BlockSpec((1, H, tm, QK_PAD), hmap),
            pl.BlockSpec((1, H, tm, QK_PAD), hmap),
            pl.BlockSpec((1, H, tm, V_HEAD), hmap),
        ],
        out_shape=[
            jax.ShapeDtypeStruct((B, S, D_POOL), F32),
            jax.ShapeDtypeStruct((B, H, S, QK_PAD), BF16),
            jax.ShapeDtypeStruct((B, H, S, QK_PAD), BF16),
            jax.ShapeDtypeStruct((B, H, S, V_HEAD), BF16),
        ],
        compiler_params=_cparams("parallel", "parallel"),
        name="mix_in",
    )(x, gain, w_in_ext, q_norm, kv_norm, w_uq_ext, w_ukv, cos, sin)


def _mla_kernel(q_ref, k_ref, v_ref, o_ref, *, S, kc):
    q = q_ref[0, 0]
    m = l = acc = None
    for c in range(S // kc):
        rows = slice(c * kc, (c + 1) * kc)
        s = _dot_nt(q, k_ref[0, 0, rows, :])
        m_c = jnp.max(s, axis=-1, keepdims=True)
        if c == 0:
            m = m_c
            p = jnp.exp2(s - m)
            l = jnp.sum(p, axis=-1, keepdims=True)
            acc = _dot(p.astype(BF16), v_ref[0, 0, rows, :])
        else:
            m_new = jnp.maximum(m, m_c)
            alpha = jnp.exp2(m - m_new)
            p = jnp.exp2(s - m_new)
            l = alpha * l + jnp.sum(p, axis=-1, keepdims=True)
            acc = alpha * acc + _dot(p.astype(BF16), v_ref[0, 0, rows, :])
            m = m_new
    o_ref[0] = (acc / l).astype(BF16)


def _mla_attn(q, k, v, tq, kc):
    B, H, S, _ = q.shape
    return pl.pallas_call(
        functools.partial(_mla_kernel, S=S, kc=kc),
        grid=(B, H, S // tq),
        in_specs=[
            pl.BlockSpec((1, 1, tq, QK_PAD), lambda b, h, i: (b, h, i, 0)),
            pl.BlockSpec((1, 1, S, QK_PAD), lambda b, h, i: (b, h, 0, 0)),
            pl.BlockSpec((1, 1, S, V_HEAD), lambda b, h, i: (b, h, 0, 0)),
        ],
        out_specs=pl.BlockSpec((1, tq, V_HEAD), lambda b, h, i: (b, i, h)),
        out_shape=jax.ShapeDtypeStruct((B, S, H * V_HEAD), BF16),
        compiler_params=_cparams("parallel", "parallel", "parallel"),
        name="mla_attn",
    )(q, k, v)


def _mix_out_kernel(x_ref, u_ref, up_ref, un_ref, wpool_ref, ps_ref, mla_ref, wout_ref,
                    nx_ref, wxq_ref, kv_ref, wxo_ref, nf_ref, wr_ref,
                    x2_ref, hnt_ref, afft_ref, ext_ref, *, S, tm, rs):
    i = pl.program_id(1)
    nt = pl.num_programs(1)
    hal = POOL_HALO
    ext_ref[0:hal, :] = jnp.where(i > 0, up_ref[0], 0.0)
    ext_ref[hal:hal + tm, :] = u_ref[0]
    ext_ref[hal + tm:2 * hal + tm, :] = jnp.where(i < nt - 1, un_ref[0], 0.0)
    for r0 in range(0, tm, rs):
        rows = slice(r0, r0 + rs)
        t = i * tm + r0 + lax.broadcasted_iota(jnp.int32, (rs, POOL_GROUP), 0)
        pooled = []
        for g, w in enumerate(POOL_WINDOWS):
            left = w // 2
            right = w - 1 - left
            lanes = slice(g * POOL_GROUP, (g + 1) * POOL_GROUP)
            acc = ext_ref[pl.ds(r0 + hal - left, rs), lanes]
            for d in range(-left + 1, right + 1):
                acc = acc + ext_ref[pl.ds(r0 + hal + d, rs), lanes]
            cnt = (jnp.clip(t + right + 1, 0, S) - jnp.clip(t - left, 0, S)).astype(F32)
            diff = (acc / cnt - ext_ref[pl.ds(r0 + hal, rs), lanes]).astype(BF16)
            pooled.append(_dot(diff, wpool_ref[0, g]) * ps_ref[0, :, lanes])
        pool_out = jnp.concatenate(pooled, axis=-1).astype(BF16)
        x1 = (x_ref[0, rows, :] + _dot(pool_out, wout_ref[0, :D_POOL, :])
              + _dot(mla_ref[0, rows, :], wout_ref[0, D_POOL:, :]))

        h = _rms(x1, nx_ref[0]).astype(BF16)
        q = (_dot(h, wxq_ref[0]) * (XHEAD ** -0.5)).astype(BF16)
        outs = []
        for hd in range(N_XHEADS):
            cols = slice(hd * XHEAD, (hd + 1) * XHEAD)
            kh = kv_ref[0, 0, :, cols]
            vh = kv_ref[0, 0, :, D_MODEL + hd * XHEAD:D_MODEL + (hd + 1) * XHEAD]
            s = _dot_nt(q[:, cols], kh)
            p = jnp.exp(s - jnp.max(s, axis=-1, keepdims=True))
            denom = jnp.sum(p, axis=-1, keepdims=True)
            outs.append((_dot(p.astype(BF16), vh) / denom).astype(BF16))
        x2 = x1 + _dot(jnp.concatenate(outs, axis=-1), wxo_ref[0])
        x2_ref[0, rows, :] = x2

        hn = _rms(x2, nf_ref[0])
        hnt_ref[0, :, rows] = hn.T.astype(BF16)
        hi = hn.astype(BF16)
        lo = (hn - hi.astype(F32)).astype(BF16)
        parts = (_dot(hi, wr_ref[0, :D_MODEL, :]) + _dot(lo, wr_ref[0, D_MODEL:, :])).T
        logits = parts[:N_EXPERTS] + parts[N_EXPERTS:2 * N_EXPERTS]
        pe = jnp.exp(logits - jnp.max(logits, axis=0, keepdims=True))
        afft_ref[0, :, rows] = pe / jnp.sum(pe, axis=0, keepdims=True)


def _mix_out(x, u, w_pool, pool_scale, mla, w_out, norm_x, wx_q, kvmem, wx_o, norm_ffn, w_router, l, tm, rs):
    B, S, D = x.shape
    M = kvmem.shape[2]
    nh = tm // POOL_HALO
    last = S // POOL_HALO - 1
    lmap = lambda b, i: (l, 0, 0)
    tmap = lambda b, i: (b, i, 0)
    return pl.pallas_call(
        functools.partial(_mix_out_kernel, S=S, tm=tm, rs=rs),
        grid=(B, S // tm),
        in_specs=[
            pl.BlockSpec((1, tm, D), tmap),
            pl.BlockSpec((1, tm, D_POOL), tmap),
            pl.BlockSpec((1, POOL_HALO, D_POOL), lambda b, i: (b, jnp.maximum(i * nh - 1, 0), 0)),
            pl.BlockSpec((1, POOL_HALO, D_POOL), lambda b, i: (b, jnp.minimum((i + 1) * nh, last), 0)),
            pl.BlockSpec((1, N_POOL_GROUPS, POOL_GROUP, POOL_GROUP), lambda b, i: (l, 0, 0, 0)),
            pl.BlockSpec((1, 1, D_POOL), lmap),
            pl.BlockSpec((1, tm, D_POOL), tmap),
            pl.BlockSpec((1, D, D), lmap),
            pl.BlockSpec((1, 1, D), lmap),
            pl.BlockSpec((1, D, D), lmap),
            pl.BlockSpec((1, 1, M, 2 * D), lambda b, i: (l, b, 0, 0)),
            pl.BlockSpec((1, D, D), lmap),
            pl.BlockSpec((1, 1, D), lmap),
            pl.BlockSpec((1, 2 * D, LANES), lmap),
        ],
        out_specs=[
            pl.BlockSpec((1, tm, D), tmap),
            pl.BlockSpec((1, D, tm), lambda b, i: (b, 0, i)),
            pl.BlockSpec((1, N_EXPERTS, tm), lambda b, i: (b, 0, i)),
        ],
        out_shape=[
            jax.ShapeDtypeStruct((B, S, D), F32),
            jax.ShapeDtypeStruct((B, D, S), BF16),
            jax.ShapeDtypeStruct((B, N_EXPERTS, S), F32),
        ],
        scratch_shapes=[pltpu.VMEM((tm + 2 * POOL_HALO, D_POOL), F32)],
        compiler_params=_cparams("parallel", "parallel"),
        name="mix_out",
    )(x, u, u, u, w_pool, pool_scale, mla, w_out, norm_x, wx_q, kvmem, wx_o, norm_ffn, w_router)


def _select_kernel(afft_ref, post_ref, cnt_ref, *, S, C):
    rt = ROUTE_TILE
    n_tiles = S // rt
    E = N_EXPERTS
    bits = lax.bitcast_convert_type(afft_ref[0], jnp.int32)

    def search(it, prefix):
        cand = prefix | jnp.left_shift(jnp.int32(1), 30 - it)
        cnt = jnp.sum(jnp.where(bits >= cand, 1.0, 0.0), axis=1, keepdims=True)
        return jnp.where(cnt >= C, cand, prefix)

    thr = lax.fori_loop(0, 31, search, jnp.zeros((E, 1), jnp.int32))
    n_gt = jnp.sum(jnp.where(bits > thr, 1.0, 0.0), axis=1, keepdims=True)
    need = C - n_gt
    r = lax.broadcasted_iota(jnp.int32, (rt, rt), 0)
    c = lax.broadcasted_iota(jnp.int32, (rt, rt), 1)
    tri = jnp.where(r <= c, 1.0, 0.0).astype(BF16)
    lane = lax.broadcasted_iota(jnp.int32, (E, LANES), 1)
    counts = jnp.zeros((E, LANES), F32)
    off_eq = jnp.zeros((E, 1), F32)
    off_sel = jnp.zeros((E, 1), F32)
    for kt in range(n_tiles):
        toks = slice(kt * rt, (kt + 1) * rt)
        bc = bits[:, toks]
        gt = bc > thr
        eq = bc == thr
        eqf = jnp.where(eq, 1.0, 0.0)
        rank = off_eq + _dot(eqf.astype(BF16), tri) - eqf
        sel = gt | (eq & (rank < need))
        self_ = jnp.where(sel, 1.0, 0.0)
        slot = off_sel + _dot(self_.astype(BF16), tri) - self_
        post_ref[0, :, toks] = jnp.where(sel, slot, -1.0)
        counts = jnp.where(lane == kt, off_sel, counts)
        off_eq = off_eq + jnp.sum(eqf, axis=1, keepdims=True)
        off_sel = off_sel + jnp.sum(self_, axis=1, keepdims=True)
    cnt_ref[0] = jnp.where(lane == n_tiles, off_sel, counts).astype(jnp.int32)


def _select(afft, C):
    B, E, S = afft.shape
    assert S // ROUTE_TILE < LANES
    return pl.pallas_call(
        functools.partial(_select_kernel, S=S, C=C),
        grid=(B,),
        in_specs=[pl.BlockSpec((1, E, S), lambda b: (b, 0, 0))],
        out_specs=[
            pl.BlockSpec((1, E, S), lambda b: (b, 0, 0)),
            pl.BlockSpec((1, E, LANES), lambda b: (b, 0, 0)),
        ],
        out_shape=[
            jax.ShapeDtypeStruct((B, E, S), F32),
            jax.ShapeDtypeStruct((B, E, LANES), jnp.int32),
        ],
        compiler_params=_cparams("parallel"),
        name="select",
    )(afft)


def _window(first_slot, C):
    return pl.multiple_of(jnp.minimum(first_slot & -LANES, C - ROUTE_TILE), LANES)


def _spills(end_slot, w0, jb):
    return (end_slot > jnp.maximum(w0 + ROUTE_TILE, jb * ROUTE_TILE)) & (w0 < jb * ROUTE_TILE)


def _onehot(prow, start, first=None):
    rel = lax.broadcasted_iota(jnp.int32, (ROUTE_TILE, prow.shape[1]), 0).astype(F32)
    hit = (prow - lax.convert_element_type(start, F32)) == rel
    if first is not None:
        hit = hit & (rel >= lax.convert_element_type(first - start, F32))
    return jnp.where(hit, 1.0, 0.0).astype(BF16)


def _moe_ffn_kernel(cnt_ref, post_ref, hnt_ref, wg_ref, wu_ref, wd_ref, yt_ref,
                    xgt_ref, wg_s, wu_s, wd_s, *, S, C, span):
    rt = ROUTE_TILE
    g = pl.program_id(0)
    b = pl.program_id(1)

    @pl.when(g < N_EXPERTS)
    def _():
        nxt = g & 1
        pr, pd = wg_ref.shape[2], wd_ref.shape[2]
        r0 = pl.multiple_of(b * pr, pr)
        d0 = pl.multiple_of(b * pd, pd)
        wg_s[nxt, pl.ds(r0, pr), :] = wg_ref[0, 0].astype(BF16)
        wu_s[nxt, pl.ds(r0, pr), :] = wu_ref[0, 0].astype(BF16)
        wd_s[nxt, pl.ds(d0, pd), :] = wd_ref[0, 0].astype(BF16)

    @pl.when(g == 0)
    def _():
        yt_ref[0, 0] = jnp.zeros(yt_ref.shape[2:], BF16)

    @pl.when(g > 0)
    def _():
        e = g - 1
        cur = e & 1
        base = (b * N_EXPERTS + e) * (S // rt + 1)
        xgt_ref[...] = jnp.zeros(xgt_ref.shape, BF16)
        n_spans = S // span
        bounds = [cnt_ref[base + t * (span // rt)] for t in range(n_spans + 1)]
        starts = [_window(bounds[t], C) for t in range(n_spans)]
        for t in range(n_spans):
            toks = slice(t * span, (t + 1) * span)
            onehot = _onehot(post_ref[0, 0, :, toks], starts[t])
            xgt_ref[:, pl.ds(starts[t], rt)] += _dot_nt(hnt_ref[0, :, toks], onehot).astype(BF16)
        spills = [(t, jb, _spills(bounds[t + 1], starts[t], jb)) for t in range(n_spans) for jb in range(1, C // rt)]

        @pl.when(functools.reduce(jnp.logical_or, [s for _, _, s in spills], False))
        def _():
            for t, jb, spill in spills:
                @pl.when(spill)
                def _():
                    toks = slice(t * span, (t + 1) * span)
                    onehot = _onehot(post_ref[0, 0, :, toks], jb * rt, first=starts[t] + rt)
                    xgt_ref[:, jb * rt:(jb + 1) * rt] += _dot_nt(hnt_ref[0, :, toks], onehot).astype(BF16)
        x = xgt_ref[...].T
        y = jnp.zeros((C, D_MODEL), F32)
        for c in range(D_EXPERT // FFN_CHUNK):
            cols = slice(c * FFN_CHUNK, (c + 1) * FFN_CHUNK)
            a = _dot(x, wg_s[cur, :, cols])
            gt = _dot(x, wu_s[cur, :, cols])
            y = y + _dot((jax.nn.silu(a) * gt).astype(BF16), wd_s[cur, cols, :])
        yt_ref[0, 0] = y.T.astype(BF16)


def _moe_ffn(cnt, post, hnt, w_gate, w_up, w_down, l, C, span):
    B, D, S = hnt.shape
    E = N_EXPERTS
    F = D_EXPERT
    assert D % B == 0 and F % B == 0 and (D // B) % 16 == 0
    wmap = lambda g, b, cnt: (l, jnp.minimum(g, E - 1), b, 0)
    emap = lambda g, b, cnt: (b, jnp.maximum(g - 1, 0), 0, 0)
    return pl.pallas_call(
        functools.partial(_moe_ffn_kernel, S=S, C=C, span=span),
        grid_spec=pltpu.PrefetchScalarGridSpec(
            num_scalar_prefetch=1,
            grid=(E + 1, B),
            in_specs=[
                pl.BlockSpec((1, 1, 1, S), emap),
                pl.BlockSpec((1, D, S), lambda g, b, cnt: (jnp.where(g == 0, 0, b), 0, 0)),
                pl.BlockSpec((1, 1, D // B, F), wmap),
                pl.BlockSpec((1, 1, D // B, F), wmap),
                pl.BlockSpec((1, 1, F // B, D), wmap),
            ],
            out_specs=pl.BlockSpec((1, 1, D, C), lambda g, b, cnt: (b, jnp.where(g == 0, E, g - 1), 0, 0)),
            scratch_shapes=[
                pltpu.VMEM((D, C), BF16),
                pltpu.VMEM((2, D, F), BF16),
                pltpu.VMEM((2, D, F), BF16),
                pltpu.VMEM((2, F, D), BF16),
            ],
        ),
        out_shape=jax.ShapeDtypeStruct((B, E + 1, D, C), BF16),
        compiler_params=_cparams("arbitrary", "arbitrary"),
        name="moe_ffn",
    )(cnt, post, hnt, w_gate, w_up, w_down)


def _combine_kernel(cnt_ref, x_ref, yt_ref, post_ref, afft_ref, gf_ref, o_ref, acc_ref, *, S, C, tt, final):
    rt = ROUTE_TILE
    n_tiles = S // rt
    b = pl.program_id(0)
    ti = pl.program_id(1)
    acc_ref[...] = jnp.zeros(acc_ref.shape, F32)
    blocks = []
    for e in range(N_EXPERTS):
        base = (b * N_EXPERTS + e) * (n_tiles + 1)
        for sub in range(tt // rt):
            tile = ti * (tt // rt) + sub
            blocks.append((e, sub, cnt_ref[base + tile + 1], _window(cnt_ref[base + tile], C)))
    for e, sub, end, w0 in blocks:
        toks = slice(sub * rt, (sub + 1) * rt)
        onehot = _onehot(post_ref[0, e:e + 1, toks], w0)
        gate = afft_ref[0, e:e + 1, toks]
        acc_ref[:, toks] += gate * _dot(yt_ref[0, e, :, pl.ds(w0, rt)], onehot)
    spills = [(blk, jb, _spills(blk[2], blk[3], jb)) for blk in blocks for jb in range(1, C // rt)]

    @pl.when(functools.reduce(jnp.logical_or, [s for _, _, s in spills], False))
    def _():
        for (e, sub, end, w0), jb, spill in spills:
            @pl.when(spill)
            def _():
                toks = slice(sub * rt, (sub + 1) * rt)
                onehot = _onehot(post_ref[0, e:e + 1, toks], jb * rt, first=w0 + rt)
                gate = afft_ref[0, e:e + 1, toks]
                acc_ref[:, toks] += gate * _dot(yt_ref[0, e, :, jb * rt:(jb + 1) * rt], onehot)

    out = x_ref[0] + acc_ref[...].T
    if final:
        out = _rms(out, gf_ref[0])
    o_ref[0] = out


def _combine(cnt, x, yt, post, afft, norm_final, C, tt, final):
    B, S, D = x.shape
    E = N_EXPERTS
    return pl.pallas_call(
        functools.partial(_combine_kernel, S=S, C=C, tt=tt, final=final),
        grid_spec=pltpu.PrefetchScalarGridSpec(
            num_scalar_prefetch=1,
            grid=(B, S // tt),
            in_specs=[
                pl.BlockSpec((1, tt, D), lambda b, t, cnt: (b, t, 0)),
                pl.BlockSpec((1, E, D, C), lambda b, t, cnt: (b, 0, 0, 0)),
                pl.BlockSpec((1, E, tt), lambda b, t, cnt: (b, 0, t)),
                pl.BlockSpec((1, E, tt), lambda b, t, cnt: (b, 0, t)),
                pl.BlockSpec((1, 1, D), lambda b, t, cnt: (0, 0, 0)),
            ],
            out_specs=pl.BlockSpec((1, tt, D), lambda b, t, cnt: (b, t, 0)),
            scratch_shapes=[pltpu.VMEM((D, tt), F32)],
        ),
        out_shape=jax.ShapeDtypeStruct((B, S, D), F32),
        compiler_params=_cparams("parallel", "parallel"),
        name="combine",
    )(cnt, x, yt, post, afft, norm_final)


def _rot_cols(w):
    half = w.shape[-1] // 2
    return jnp.concatenate([-w[..., half:], w[..., :half]], axis=-1)


def _pad_lanes(w):
    return jnp.concatenate([w, jnp.zeros(w.shape[:-1] + (LANES - w.shape[-1],), w.dtype)], axis=-1)


def kernel(x, mem, positions, norm_mix, w_in, q_norm, kv_norm, w_uq, w_ukv, w_pool, pool_scale,
           w_out, norm_x, mem_norm, wx_q, wx_kv, wx_o, norm_ffn, w_router, w_gate, w_up, w_down,
           norm_final):
    B, S, D = x.shape
    L = w_in.shape[0]
    C = CAPACITY_FACTOR * S // N_EXPERTS
    assert D == D_MODEL and S % 1024 == 0 and C % ROUTE_TILE == 0
    tm = 512
    tmo = 512
    rs = 256
    tq = 1024
    kc = 1024
    tt = 512
    span = 512

    half = QK_ROPE // 2
    inv_freq = ROPE_THETA ** (-jnp.arange(half, dtype=F32) / half)
    ang = positions.astype(F32)[..., None] * inv_freq
    cos = _pad_lanes(jnp.concatenate([jnp.cos(ang)] * 2, axis=-1))
    sin = _pad_lanes(jnp.concatenate([jnp.sin(ang)] * 2, axis=-1))

    k_r = w_in[..., D_IN - QK_ROPE:]
    w_in_ext = jnp.concatenate(
        [w_in[..., :D_IN - QK_ROPE], _pad_lanes(k_r), _pad_lanes(_rot_cols(k_r))], axis=-1).astype(BF16)
    wq = w_uq.reshape(L, Q_LORA, N_MLA_HEADS, QK_NOPE + QK_ROPE)
    q_r = wq[..., QK_NOPE:]
    w_uq_ext = jnp.concatenate([wq[..., :QK_NOPE], _pad_lanes(q_r), _pad_lanes(_rot_cols(q_r))], axis=-1)
    w_uq_ext = w_uq_ext.reshape(L, Q_LORA, N_MLA_HEADS * Q_HEAD_EXT).astype(BF16)
    wr_hi = w_router.astype(BF16)
    wr_lo = (w_router - wr_hi.astype(F32)).astype(BF16)
    w_router_p = jnp.concatenate(
        [_pad_lanes(jnp.concatenate([wr_hi, wr_lo], axis=-1)), _pad_lanes(wr_hi)], axis=1)
    row = lambda g: g.reshape(L, 1, g.shape[-1])
    norm_mix, q_norm, kv_norm, pool_scale, norm_x, mem_norm, norm_ffn = map(
        row, (norm_mix, q_norm, kv_norm, pool_scale, norm_x, mem_norm, norm_ffn))
    norm_final = norm_final.reshape(1, 1, D)
    w_ukv, w_pool, w_out, wx_q, wx_kv, wx_o = (
        w.astype(BF16) for w in (w_ukv, w_pool, w_out, wx_q, wx_kv, wx_o))

    kvmem = _memkv(mem, mem_norm, wx_kv)

    n_t1 = S // ROUTE_TILE + 1
    for l in range(L):
        u, q, k, v = _mix_in(x, norm_mix, w_in_ext, q_norm, kv_norm, w_uq_ext, w_ukv, cos, sin, l, tm)
        mla = _mla_attn(q, k, v, tq, kc)
        x, hnt, afft = _mix_out(x, u, w_pool, pool_scale, mla, w_out, norm_x, wx_q, kvmem, wx_o,
                                norm_ffn, w_router_p, l, tmo, rs)
        post, cnt = _select(afft, C)
        cnt = cnt[:, :, :n_t1].reshape(-1)
        post_e = post.reshape(B, N_EXPERTS, 1, S)
        yt = _moe_ffn(cnt, post_e, hnt, w_gate, w_up, w_down, l, C, span)
        x = _combine(cnt, x, yt, post, afft, norm_final, C, tt, final=(l == L - 1))
    return x
```

```python
import functools

import jax
import jax.numpy as jnp
from jax import lax
from jax.experimental import pallas as pl
from jax.experimental.pallas import tpu as pltpu

F32 = jnp.float32
BF16 = jnp.bfloat16

D_MODEL = 1024
D_POOL = 512
N_POOL_GROUPS = 4
POOL_GROUP = 128
POOL_WINDOWS = (2, 4, 8, 16)
POOL_HALO = 8
N_MLA_HEADS = 4
QK_NOPE = 128
QK_ROPE = 64
V_HEAD = 128
Q_LORA = 256
KV_LORA = 128
D_IN = D_POOL + Q_LORA + KV_LORA + QK_ROPE
ROPE_THETA = 10000.0
N_XHEADS = 4
XHEAD = D_MODEL // N_XHEADS
N_EXPERTS = 16
CAPACITY_FACTOR = 2
D_EXPERT = 2 * D_MODEL
EPS = 1e-6

LANES = 128
QK_PAD = 256
D_IN_EXT = D_POOL + Q_LORA + KV_LORA + 2 * LANES
Q_HEAD_EXT = QK_NOPE + 2 * LANES
ROUTE_TILE = 256
FFN_CHUNK = 512
V7X_VMEM_LIMIT = 56 * 1024 * 1024
LOG2_E = 1.4426950408889634


def _cparams(*sem):
    return pltpu.CompilerParams(dimension_semantics=sem, vmem_limit_bytes=V7X_VMEM_LIMIT)


def _rms(t, gain):
    return t * lax.rsqrt(jnp.mean(t * t, axis=-1, keepdims=True) + EPS) * gain


def _dot(a, b):
    return jnp.dot(a, b, preferred_element_type=F32)


def _dot_nt(a, b):
    return lax.dot_general(a, b, (((1,), (1,)), ((), ())), preferred_element_type=F32)


def _memkv_kernel(mem_ref, gain_ref, w_ref, o_ref):
    h = _rms(mem_ref[0], gain_ref[0])
    o_ref[0, 0] = _dot(h.astype(BF16), w_ref[0]).astype(BF16)


def _memkv(mem, mem_norm, wx_kv):
    B, M, D = mem.shape
    L = wx_kv.shape[0]
    return pl.pallas_call(
        _memkv_kernel,
        grid=(L, B),
        in_specs=[
            pl.BlockSpec((1, M, D), lambda l, b: (b, 0, 0)),
            pl.BlockSpec((1, 1, D), lambda l, b: (l, 0, 0)),
            pl.BlockSpec((1, D, 2 * D), lambda l, b: (l, 0, 0)),
        ],
        out_specs=pl.BlockSpec((1, 1, M, 2 * D), lambda l, b: (l, b, 0, 0)),
        out_shape=jax.ShapeDtypeStruct((L, B, M, 2 * D), BF16),
        compiler_params=_cparams("parallel", "parallel"),
        name="memkv",
    )(mem, mem_norm, wx_kv)


def _mix_in_kernel(x_ref, g_ref, win_ref, qn_ref, kvn_ref, wuq_ref, wukv_ref, cos_ref, sin_ref,
                   u_ref, q_ref, k_ref, v_ref):
    h = _rms(x_ref[0], g_ref[0]).astype(BF16)
    proj = _dot(h, win_ref[0])
    u_ref[0] = proj[:, :D_POOL]
    cos = cos_ref[0]
    sin = sin_ref[0]
    o = D_POOL
    cq = _rms(proj[:, o:o + Q_LORA], qn_ref[0]).astype(BF16)
    o += Q_LORA
    ckv = _rms(proj[:, o:o + KV_LORA], kvn_ref[0]).astype(BF16)
    o += KV_LORA
    k_rope = (proj[:, o:o + LANES] * cos + proj[:, o + LANES:o + 2 * LANES] * sin).astype(BF16)
    qall = _dot(cq, wuq_ref[0])
    kvall = _dot(ckv, wukv_ref[0])
    scale = (QK_NOPE + QK_ROPE) ** -0.5 * LOG2_E
    for hd in range(N_MLA_HEADS):
        qb = hd * Q_HEAD_EXT
        q_ref[0, hd, :, :QK_NOPE] = (qall[:, qb:qb + QK_NOPE] * scale).astype(BF16)
        roped = qall[:, qb + QK_NOPE:qb + QK_NOPE + LANES] * cos + qall[:, qb + QK_NOPE + LANES:qb + Q_HEAD_EXT] * sin
        q_ref[0, hd, :, QK_NOPE:] = (roped * scale).astype(BF16)
        kb = hd * (QK_NOPE + V_HEAD)
        k_ref[0, hd, :, :QK_NOPE] = kvall[:, kb:kb + QK_NOPE].astype(BF16)
        k_ref[0, hd, :, QK_NOPE:] = k_rope
        v_ref[0, hd] = kvall[:, kb + QK_NOPE:kb + QK_NOPE + V_HEAD].astype(BF16)


def _mix_in(x, gain, w_in_ext, q_norm, kv_norm, w_uq_ext, w_ukv, cos, sin, l, tm):
    B, S, D = x.shape
    H = N_MLA_HEADS
    lmap = lambda b, i: (l, 0, 0)
    tmap = lambda b, i: (b, i, 0)
    hmap = lambda b, i: (b, 0, i, 0)
    return pl.pallas_call(
        _mix_in_kernel,
        grid=(B, S // tm),
        in_specs=[
            pl.BlockSpec((1, tm, D), tmap),
            pl.BlockSpec((1, 1, D), lmap),
            pl.BlockSpec((1, D, D_IN_EXT), lmap),
            pl.BlockSpec((1, 1, Q_LORA), lmap),
            pl.BlockSpec((1, 1, KV_LORA), lmap),
            pl.BlockSpec((1, Q_LORA, H * Q_HEAD_EXT), lmap),
            pl.BlockSpec((1, KV_LORA, H * (QK_NOPE + V_HEAD)), lmap),
            pl.BlockSpec((1, tm, LANES), tmap),
            pl.BlockSpec((1, tm, LANES), tmap),
        ],
        out_specs=[
            pl.BlockSpec((1, tm, D_POOL), tmap),
            pl.BlockSpec((1, H, tm, QK_PAD), hmap),
            pl.BlockSpec((1, H, tm, QK_PAD), hmap),
            pl.BlockSpec((1, H, tm, V_HEAD), hmap),
        ],
        out_shape=[
            jax.ShapeDtypeStruct((B, S, D_POOL), F32),
            jax.ShapeDtypeStruct((B, H, S, QK_PAD), BF16),
            jax.ShapeDtypeStruct((B, H, S, QK_PAD), BF16),
            jax.ShapeDtypeStruct((B, H, S, V_HEAD), BF16),
        ],
        compiler_params=_cparams("parallel", "parallel"),
        name="mix_in",
    )(x, gain, w_in_ext, q_norm, kv_norm, w_uq_ext, w_ukv, cos, sin)


def _mla_kernel(q_ref, k_ref, v_ref, o_ref, *, S, kc):
    q = q_ref[0, 0]
    m = l = acc = None
    for c in range(S // kc):
        rows = slice(c * kc, (c + 1) * kc)
        s = _dot_nt(q, k_ref[0, 0, rows, :])
        m_c = jnp.max(s, axis=-1, keepdims=True)
        if c == 0:
            m = m_c
            p = jnp.exp2(s - m)
            l = jnp.sum(p, axis=-1, keepdims=True)
            acc = _dot(p.astype(BF16), v_ref[0, 0, rows, :])
        else:
            m_new = jnp.maximum(m, m_c)
            alpha = jnp.exp2(m - m_new)
            p = jnp.exp2(s - m_new)
            l = alpha * l + jnp.sum(p, axis=-1, keepdims=True)
            acc = alpha * acc + _dot(p.astype(BF16), v_ref[0, 0, rows, :])
            m = m_new
    o_ref[0] = (acc / l).astype(BF16)


def _mla_attn(q, k, v, tq, kc):
    B, H, S, _ = q.shape
    return pl.pallas_call(
        functools.partial(_mla_kernel, S=S, kc=kc),
        grid=(B, H, S // tq),
        in_specs=[
            pl.BlockSpec((1, 1, tq, QK_PAD), lambda b, h, i: (b, h, i, 0)),
            pl.BlockSpec((1, 1, S, QK_PAD), lambda b, h, i: (b, h, 0, 0)),
            pl.BlockSpec((1, 1, S, V_HEAD), lambda b, h, i: (b, h, 0, 0)),
        ],
        out_specs=pl.BlockSpec((1, tq, V_HEAD), lambda b, h, i: (b, i, h)),
        out_shape=jax.ShapeDtypeStruct((B, S, H * V_HEAD), BF16),
        compiler_params=_cparams("parallel", "parallel", "parallel"),
        name="mla_attn",
    )(q, k, v)


def _mix_out_kernel(x_ref, u_ref, up_ref, un_ref, wpool_ref, ps_ref, mla_ref, wout_ref,
                    nx_ref, wxq_ref, kv_ref, wxo_ref, nf_ref, wr_ref,
                    x2_ref, hnt_ref, afft_ref, ext_ref, *, S, tm, rs):
    i = pl.program_id(1)
    nt = pl.num_programs(1)
    hal = POOL_HALO
    ext_ref[0:hal, :] = jnp.where(i > 0, up_ref[0], 0.0)
    ext_ref[hal:hal + tm, :] = u_ref[0]
    ext_ref[hal + tm:2 * hal + tm, :] = jnp.where(i < nt - 1, un_ref[0], 0.0)
    for r0 in range(0, tm, rs):
        rows = slice(r0, r0 + rs)
        t = i * tm + r0 + lax.broadcasted_iota(jnp.int32, (rs, POOL_GROUP), 0)
        pooled = []
        for g, w in enumerate(POOL_WINDOWS):
            left = w // 2
            right = w - 1 - left
            lanes = slice(g * POOL_GROUP, (g + 1) * POOL_GROUP)
            acc = ext_ref[pl.ds(r0 + hal - left, rs), lanes]
            for d in range(-left + 1, right + 1):
                acc = acc + ext_ref[pl.ds(r0 + hal + d, rs), lanes]
            cnt = (jnp.clip(t + right + 1, 0, S) - jnp.clip(t - left, 0, S)).astype(F32)
            diff = (acc / cnt - ext_ref[pl.ds(r0 + hal, rs), lanes]).astype(BF16)
            pooled.append(_dot(diff, wpool_ref[0, g]) * ps_ref[0, :, lanes])
        pool_out = jnp.concatenate(pooled, axis=-1).astype(BF16)
        x1 = (x_ref[0, rows, :] + _dot(pool_out, wout_ref[0, :D_POOL, :])
              + _dot(mla_ref[0, rows, :], wout_ref[0, D_POOL:, :]))

        h = _rms(x1, nx_ref[0]).astype(BF16)
        q = (_dot(h, wxq_ref[0]) * (XHEAD ** -0.5)).astype(BF16)
        outs = []
        for hd in range(N_XHEADS):
            cols = slice(hd * XHEAD, (hd + 1) * XHEAD)
            kh = kv_ref[0, 0, :, cols]
            vh = kv_ref[0, 0, :, D_MODEL + hd * XHEAD:D_MODEL + (hd + 1) * XHEAD]
            s = _dot_nt(q[:, cols], kh)
            p = jnp.exp(s - jnp.max(s, axis=-1, keepdims=True))
            denom = jnp.sum(p, axis=-1, keepdims=True)
            outs.append((_dot(p.astype(BF16), vh) / denom).astype(BF16))
        x2 = x1 + _dot(jnp.concatenate(outs, axis=-1), wxo_ref[0])
        x2_ref[0, rows, :] = x2

        hn = _rms(x2, nf_ref[0])
        hi = hn.astype(BF16)
        hnt_ref[0, :, rows] = hi.T
        lo = (hn - hi.astype(F32)).astype(BF16)
        parts = (_dot(hi, wr_ref[0, :D_MODEL, :]) + _dot(lo, wr_ref[0, D_MODEL:, :])).T
        logits = parts[:N_EXPERTS] + parts[N_EXPERTS:2 * N_EXPERTS]
        pe = jnp.exp(logits - jnp.max(logits, axis=0, keepdims=True))
        afft_ref[0, :, rows] = pe / jnp.sum(pe, axis=0, keepdims=True)


def _mix_out(x, u, w_pool, pool_scale, mla, w_out, norm_x, wx_q, kvmem, wx_o, norm_ffn, w_router, l, tm, rs):
    B, S, D = x.shape
    M = kvmem.shape[2]
    nh = tm // POOL_HALO
    last = S // POOL_HALO - 1
    lmap = lambda b, i: (l, 0, 0)
    tmap = lambda b, i: (b, i, 0)
    return pl.pallas_call(
        functools.partial(_mix_out_kernel, S=S, tm=tm, rs=rs),
        grid=(B, S // tm),
        in_specs=[
            pl.BlockSpec((1, tm, D), tmap),
            pl.BlockSpec((1, tm, D_POOL), tmap),
            pl.BlockSpec((1, POOL_HALO, D_POOL), lambda b, i: (b, jnp.maximum(i * nh - 1, 0), 0)),
            pl.BlockSpec((1, POOL_HALO, D_POOL), lambda b, i: (b, jnp.minimum((i + 1) * nh, last), 0)),
            pl.BlockSpec((1, N_POOL_GROUPS, POOL_GROUP, POOL_GROUP), lambda b, i: (l, 0, 0, 0)),
            pl.BlockSpec((1, 1, D_POOL), lmap),
            pl.BlockSpec((1, tm, D_POOL), tmap),
            pl.BlockSpec((1, D, D), lmap),
            pl.BlockSpec((1, 1, D), lmap),
            pl.BlockSpec((1, D, D), lmap),
            pl.BlockSpec((1, 1, M, 2 * D), lambda b, i: (l, b, 0, 0)),
            pl.BlockSpec((1, D, D), lmap),
            pl.BlockSpec((1, 1, D), lmap),
            pl.BlockSpec((1, 2 * D, LANES), lmap),
        ],
        out_specs=[
            pl.BlockSpec((1, tm, D), tmap),
            pl.BlockSpec((1, D, tm), lambda b, i: (b, 0, i)),
            pl.BlockSpec((1, N_EXPERTS, tm), lambda b, i: (b, 0, i)),
        ],
        out_shape=[
            jax.ShapeDtypeStruct((B, S, D), F32),
            jax.ShapeDtypeStruct((B, D, S), BF16),
            jax.ShapeDtypeStruct((B, N_EXPERTS, S), F32),
        ],
        scratch_shapes=[pltpu.VMEM((tm + 2 * POOL_HALO, D_POOL), F32)],
        compiler_params=_cparams("parallel", "parallel"),
        name="mix_out",
    )(x, u, u, u, w_pool, pool_scale, mla, w_out, norm_x, wx_q, kvmem, wx_o, norm_ffn, w_router)


def _select_kernel(afft_ref, post_ref, cnt_ref, *, S, C):
    rt = ROUTE_TILE
    n_tiles = S // rt
    E = N_EXPERTS
    bits = lax.bitcast_convert_type(afft_ref[0], jnp.int32)

    def search(it, prefix):
        cand = prefix | jnp.left_shift(jnp.int32(1), 30 - it)
        cnt = jnp.sum(jnp.where(bits >= cand, 1.0, 0.0), axis=1, keepdims=True)
        return jnp.where(cnt >= C, cand, prefix)

    thr = lax.fori_loop(0, 31, search, jnp.zeros((E, 1), jnp.int32))
    n_gt = jnp.sum(jnp.where(bits > thr, 1.0, 0.0), axis=1, keepdims=True)
    need = C - n_gt
    r = lax.broadcasted_iota(jnp.int32, (rt, rt), 0)
    c = lax.broadcasted_iota(jnp.int32, (rt, rt), 1)
    tri = jnp.where(r <= c, 1.0, 0.0).astype(BF16)
    lane = lax.broadcasted_iota(jnp.int32, (E, LANES), 1)
    counts = jnp.zeros((E, LANES), F32)
    off_eq = jnp.zeros((E, 1), F32)
    off_sel = jnp.zeros((E, 1), F32)
    for kt in range(n_tiles):
        toks = slice(kt * rt, (kt + 1) * rt)
        bc = bits[:, toks]
        gt = bc > thr
        eq = bc == thr
        eqf = jnp.where(eq, 1.0, 0.0)
        rank = off_eq + _dot(eqf.astype(BF16), tri) - eqf
        sel = gt | (eq & (rank < need))
        self_ = jnp.where(sel, 1.0, 0.0)
        slot = off_sel + _dot(self_.astype(BF16), tri) - self_
        post_ref[0, :, toks] = jnp.where(sel, slot, -1.0)
        counts = jnp.where(lane == kt, off_sel, counts)
        off_eq = off_eq + jnp.sum(eqf, axis=1, keepdims=True)
        off_sel = off_sel + jnp.sum(self_, axis=1, keepdims=True)
    cnt_ref[0] = jnp.where(lane == n_tiles, off_sel, counts).astype(jnp.int32)


def _select(afft, C):
    B, E, S = afft.shape
    assert S // ROUTE_TILE < LANES
    return pl.pallas_call(
        functools.partial(_select_kernel, S=S, C=C),
        grid=(B,),
        in_specs=[pl.BlockSpec((1, E, S), lambda b: (b, 0, 0))],
        out_specs=[
            pl.BlockSpec((1, E, S), lambda b: (b, 0, 0)),
            pl.BlockSpec((1, E, LANES), lambda b: (b, 0, 0)),
        ],
        out_shape=[
            jax.ShapeDtypeStruct((B, E, S), F32),
            jax.ShapeDtypeStruct((B, E, LANES), jnp.int32),
        ],
        compiler_params=_cparams("parallel"),
        name="select",
    )(afft)


def _window(first_slot, C):
    return pl.multiple_of(jnp.minimum(first_slot & -LANES, C - ROUTE_TILE), LANES)


def _spills(end_slot, w0, jb):
    return (end_slot > jnp.maximum(w0 + ROUTE_TILE, jb * ROUTE_TILE)) & (w0 < jb * ROUTE_TILE)


def _onehot(prow, start, first=None):
    rel = lax.broadcasted_iota(jnp.int32, (ROUTE_TILE, prow.shape[1]), 0).astype(F32)
    hit = (prow - lax.convert_element_type(start, F32)) == rel
    if first is not None:
        hit = hit & (rel >= lax.convert_element_type(first - start, F32))
    return jnp.where(hit, 1.0, 0.0).astype(BF16)


def _moe_ffn_kernel(cnt_ref, post_ref, hnt_ref, wg_ref, wu_ref, wd_ref, yt_ref,
                    xgt_ref, wg_s, wu_s, wd_s, *, S, C, span):
    rt = ROUTE_TILE
    g = pl.program_id(0)
    b = pl.program_id(1)

    @pl.when(g < N_EXPERTS)
    def _():
        nxt = g & 1
        pr, pd = wg_ref.shape[2], wd_ref.shape[2]
        r0 = pl.multiple_of(b * pr, pr)
        d0 = pl.multiple_of(b * pd, pd)
        wg_s[nxt, pl.ds(r0, pr), :] = wg_ref[0, 0].astype(BF16)
        wu_s[nxt, pl.ds(r0, pr), :] = wu_ref[0, 0].astype(BF16)
        wd_s[nxt, pl.ds(d0, pd), :] = wd_ref[0, 0].astype(BF16)

    @pl.when(g == 0)
    def _():
        yt_ref[0, 0] = jnp.zeros(yt_ref.shape[2:], BF16)

    @pl.when(g > 0)
    def _():
        e = g - 1
        cur = e & 1
        base = (b * N_EXPERTS + e) * (S // rt + 1)
        xgt_ref[...] = jnp.zeros(xgt_ref.shape, BF16)
        n_spans = S // span
        bounds = [cnt_ref[base + t * (span // rt)] for t in range(n_spans + 1)]
        starts = [_window(bounds[t], C) for t in range(n_spans)]
        for t in range(n_spans):
            toks = slice(t * span, (t + 1) * span)
            onehot = _onehot(post_ref[0, 0, :, toks], starts[t])
            xgt_ref[:, pl.ds(starts[t], rt)] += _dot_nt(hnt_ref[0, :, toks], onehot).astype(BF16)
        spills = [(t, jb, _spills(bounds[t + 1], starts[t], jb)) for t in range(n_spans) for jb in range(1, C // rt)]

        @pl.when(functools.reduce(jnp.logical_or, [s for _, _, s in spills], False))
        def _():
            for t, jb, spill in spills:
                @pl.when(spill)
                def _():
                    toks = slice(t * span, (t + 1) * span)
                    onehot = _onehot(post_ref[0, 0, :, toks], jb * rt, first=starts[t] + rt)
                    xgt_ref[:, jb * rt:(jb + 1) * rt] += _dot_nt(hnt_ref[0, :, toks], onehot).astype(BF16)
        x = xgt_ref[...].T
        y = jnp.zeros((C, D_MODEL), F32)
        for c in range(D_EXPERT // FFN_CHUNK):
            cols = slice(c * FFN_CHUNK, (c + 1) * FFN_CHUNK)
            a = _dot(x, wg_s[cur, :, cols])
            gt = _dot(x, wu_s[cur, :, cols])
            y = y + _dot((jax.nn.silu(a) * gt).astype(BF16), wd_s[cur, cols, :])
        yt_ref[0, 0] = y.astype(BF16).T


def _moe_ffn(cnt, post, hnt, w_gate, w_up, w_down, l, C, span):
    B, D, S = hnt.shape
    E = N_EXPERTS
    F = D_EXPERT
    assert D % B == 0 and F % B == 0 and (D // B) % 16 == 0
    wmap = lambda g, b, cnt: (l, jnp.minimum(g, E - 1), b, 0)
    emap = lambda g, b, cnt: (b, jnp.maximum(g - 1, 0), 0, 0)
    return pl.pallas_call(
        functools.partial(_moe_ffn_kernel, S=S, C=C, span=span),
        grid_spec=pltpu.PrefetchScalarGridSpec(
            num_scalar_prefetch=1,
            grid=(E + 1, B),
            in_specs=[
                pl.BlockSpec((1, 1, 1, S), emap),
                pl.BlockSpec((1, D, S), lambda g, b, cnt: (jnp.where(g == 0, 0, b), 0, 0)),
                pl.BlockSpec((1, 1, D // B, F), wmap),
                pl.BlockSpec((1, 1, D // B, F), wmap),
                pl.BlockSpec((1, 1, F // B, D), wmap),
            ],
            out_specs=pl.BlockSpec((1, 1, D, C), lambda g, b, cnt: (b, jnp.where(g == 0, E, g - 1), 0, 0)),
            scratch_shapes=[
                pltpu.VMEM((D, C), BF16),
                pltpu.VMEM((2, D, F), BF16),
                pltpu.VMEM((2, D, F), BF16),
                pltpu.VMEM((2, F, D), BF16),
            ],
        ),
        out_shape=jax.ShapeDtypeStruct((B, E + 1, D, C), BF16),
        compiler_params=_cparams("arbitrary", "arbitrary"),
        name="moe_ffn",
    )(cnt, post, hnt, w_gate, w_up, w_down)


def _combine_kernel(cnt_ref, x_ref, yt_ref, post_ref, afft_ref, gf_ref, o_ref, acc_ref, *, S, C, tt, final):
    rt = ROUTE_TILE
    n_tiles = S // rt
    b = pl.program_id(0)
    ti = pl.program_id(1)
    acc_ref[...] = jnp.zeros(acc_ref.shape, F32)
    blocks = []
    for e in range(N_EXPERTS):
        base = (b * N_EXPERTS + e) * (n_tiles + 1)
        for sub in range(tt // rt):
            tile = ti * (tt // rt) + sub
            blocks.append((e, sub, cnt_ref[base + tile + 1], _window(cnt_ref[base + tile], C)))
    for e, sub, end, w0 in blocks:
        toks = slice(sub * rt, (sub + 1) * rt)
        onehot = _onehot(post_ref[0, e:e + 1, toks], w0)
        gate = afft_ref[0, e:e + 1, toks]
        acc_ref[:, toks] += gate * _dot(yt_ref[0, e, :, pl.ds(w0, rt)], onehot)
    spills = [(blk, jb, _spills(blk[2], blk[3], jb)) for blk in blocks for jb in range(1, C // rt)]

    @pl.when(functools.reduce(jnp.logical_or, [s for _, _, s in spills], False))
    def _():
        for (e, sub, end, w0), jb, spill in spills:
            @pl.when(spill)
            def _():
                toks = slice(sub * rt, (sub + 1) * rt)
                onehot = _onehot(post_ref[0, e:e + 1, toks], jb * rt, first=w0 + rt)
                gate = afft_ref[0, e:e + 1, toks]
                acc_ref[:, toks] += gate * _dot(yt_ref[0, e, :, jb * rt:(jb + 1) * rt], onehot)

    out = x_ref[0] + acc_ref[...].T
    if final:
        out = _rms(out, gf_ref[0])
    o_ref[0] = out


def _combine(cnt, x, yt, post, afft, norm_final, C, tt, final):
    B, S, D = x.shape
    E = N_EXPERTS
    return pl.pallas_call(
        functools.partial(_combine_kernel, S=S, C=C, tt=tt, final=final),
        grid_spec=pltpu.PrefetchScalarGridSpec(
            num_scalar_prefetch=1,
            grid=(B, S // tt),
            in_specs=[
                pl.BlockSpec((1, tt, D), lambda b, t, cnt: (b, t, 0)),
                pl.BlockSpec((1, E, D, C), lambda b, t, cnt: (b, 0, 0, 0)),
                pl.BlockSpec((1, E, tt), lambda b, t, cnt: (b, 0, t)),
                pl.BlockSpec((1, E, tt), lambda b, t, cnt: (b, 0, t)),
                pl.BlockSpec((1, 1, D), lambda b, t, cnt: (0, 0, 0)),
            ],
            out_specs=pl.BlockSpec((1, tt, D), lambda b, t, cnt: (b, t, 0)),
            scratch_shapes=[pltpu.VMEM((D, tt), F32)],
        ),
        out_shape=jax.ShapeDtypeStruct((B, S, D), F32),
        compiler_params=_cparams("parallel", "parallel"),
        name="combine",
    )(cnt, x, yt, post, afft, norm_final)


def _rot_cols(w):
    half = w.shape[-1] // 2
    return jnp.concatenate([-w[..., half:], w[..., :half]], axis=-1)


def _pad_lanes(w):
    return jnp.concatenate([w, jnp.zeros(w.shape[:-1] + (LANES - w.shape[-1],), w.dtype)], axis=-1)


def kernel(x, mem, positions, norm_mix, w_in, q_norm, kv_norm, w_uq, w_ukv, w_pool, pool_scale,
           w_out, norm_x, mem_norm, wx_q, wx_kv, wx_o, norm_ffn, w_router, w_gate, w_up, w_down,
           norm_final):
    B, S, D = x.shape
    L = w_in.shape[0]
    C = CAPACITY_FACTOR * S // N_EXPERTS
    assert D == D_MODEL and S % 1024 == 0 and C % ROUTE_TILE == 0
    tm = 1024
    tmo = 1024
    rs = 256
    tq = 1024
    kc = 1024
    tt = 512
    span = 512

    half = QK_ROPE // 2
    inv_freq = ROPE_THETA ** (-jnp.arange(half, dtype=F32) / half)
    ang = positions.astype(F32)[..., None] * inv_freq
    cos = _pad_lanes(jnp.concatenate([jnp.cos(ang)] * 2, axis=-1))
    sin = _pad_lanes(jnp.concatenate([jnp.sin(ang)] * 2, axis=-1))

    k_r = w_in[..., D_IN - QK_ROPE:]
    w_in_ext = jnp.concatenate(
        [w_in[..., :D_IN - QK_ROPE], _pad_lanes(k_r), _pad_lanes(_rot_cols(k_r))], axis=-1).astype(BF16)
    wq = w_uq.reshape(L, Q_LORA, N_MLA_HEADS, QK_NOPE + QK_ROPE)
    q_r = wq[..., QK_NOPE:]
    w_uq_ext = jnp.concatenate([wq[..., :QK_NOPE], _pad_lanes(q_r), _pad_lanes(_rot_cols(q_r))], axis=-1)
    w_uq_ext = w_uq_ext.reshape(L, Q_LORA, N_MLA_HEADS * Q_HEAD_EXT).astype(BF16)
    wr_hi = w_router.astype(BF16)
    wr_lo = (w_router - wr_hi.astype(F32)).astype(BF16)
    w_router_p = jnp.concatenate(
        [_pad_lanes(jnp.concatenate([wr_hi, wr_lo], axis=-1)), _pad_lanes(wr_hi)], axis=1)
    row = lambda g: g.reshape(L, 1, g.shape[-1])
    norm_mix, q_norm, kv_norm, pool_scale, norm_x, mem_norm, norm_ffn = map(
        row, (norm_mix, q_norm, kv_norm, pool_scale, norm_x, mem_norm, norm_ffn))
    norm_final = norm_final.reshape(1, 1, D)
    w_ukv, w_pool, w_out, wx_q, wx_kv, wx_o = (
        w.astype(BF16) for w in (w_ukv, w_pool, w_out, wx_q, wx_kv, wx_o))

    kvmem = _memkv(mem, mem_norm, wx_kv)

    n_t1 = S // ROUTE_TILE + 1
    for l in range(L):
        u, q, k, v = _mix_in(x, norm_mix, w_in_ext, q_norm, kv_norm, w_uq_ext, w_ukv, cos, sin, l, tm)
        mla = _mla_attn(q, k, v, tq, kc)
        x, hnt, afft = _mix_out(x, u, w_pool, pool_scale, mla, w_out, norm_x, wx_q, kvmem, wx_o,
                                norm_ffn, w_router_p, l, tmo, rs)
        post, cnt = _select(afft, C)
        cnt = cnt[:, :, :n_t1].reshape(-1)
        post_e = post.reshape(B, N_EXPERTS, 1, S)
        yt = _moe_ffn(cnt, post_e, hnt, w_gate, w_up, w_down, l, C, span)
        x = _combine(cnt, x, yt, post, afft, norm_final, C, tt, final=(l == L - 1))
    return x
```

```python
import functools

import jax
import jax.numpy as jnp
from jax import lax
from jax.experimental import pallas as pl
from jax.experimental.pallas import tpu as pltpu

F32 = jnp.float32
BF16 = jnp.bfloat16

D_MODEL = 1024
D_POOL = 512
N_POOL_GROUPS = 4
POOL_GROUP = 128
POOL_WINDOWS = (2, 4, 8, 16)
POOL_HALO = 8
N_MLA_HEADS = 4
QK_NOPE = 128
QK_ROPE = 64
V_HEAD = 128
Q_LORA = 256
KV_LORA = 128
D_IN = D_POOL + Q_LORA + KV_LORA + QK_ROPE
ROPE_THETA = 10000.0
N_XHEADS = 4
XHEAD = D_MODEL // N_XHEADS
N_EXPERTS = 16
CAPACITY_FACTOR = 2
D_EXPERT = 2 * D_MODEL
EPS = 1e-6

LANES = 128
QK_PAD = 256
D_IN_EXT = D_POOL + Q_LORA + KV_LORA + 2 * LANES
Q_HEAD_EXT = QK_NOPE + 2 * LANES
ROUTE_TILE = 256
FFN_CHUNK = 512
V7X_VMEM_LIMIT = 56 * 1024 * 1024
LOG2_E = 1.4426950408889634


def _cparams(*sem):
    return pltpu.CompilerParams(dimension_semantics=sem, vmem_limit_bytes=V7X_VMEM_LIMIT)


def _rms(t, gain):
    return t * lax.rsqrt(jnp.mean(t * t, axis=-1, keepdims=True) + EPS) * gain


def _dot(a, b):
    return jnp.dot(a, b, preferred_element_type=F32)


def _dot_nt(a, b):
    return lax.dot_general(a, b, (((1,), (1,)), ((), ())), preferred_element_type=F32)


def _memkv_kernel(mem_ref, gain_ref, w_ref, o_ref):
    h = _rms(mem_ref[0], gain_ref[0])
    o_ref[0, 0] = _dot(h.astype(BF16), w_ref[0]).astype(BF16)


def _memkv(mem, mem_norm, wx_kv):
    B, M, D = mem.shape
    L = wx_kv.shape[0]
    return pl.pallas_call(
        _memkv_kernel,
        grid=(L, B),
        in_specs=[
            pl.BlockSpec((1, M, D), lambda l, b: (b, 0, 0)),
            pl.BlockSpec((1, 1, D), lambda l, b: (l, 0, 0)),
            pl.BlockSpec((1, D, 2 * D), lambda l, b: (l, 0, 0)),
        ],
        out_specs=pl.BlockSpec((1, 1, M, 2 * D), lambda l, b: (l, b, 0, 0)),
        out_shape=jax.ShapeDtypeStruct((L, B, M, 2 * D), BF16),
        compiler_params=_cparams("parallel", "parallel"),
        name="memkv",
    )(mem, mem_norm, wx_kv)


def _mix_in_kernel(x_ref, g_ref, win_ref, qn_ref, kvn_ref, wuq_ref, wukv_ref, cos_ref, sin_ref,
                   u_ref, q_ref, k_ref, v_ref):
    h = _rms(x_ref[0], g_ref[0]).astype(BF16)
    proj = _dot(h, win_ref[0])
    u_ref[0] = proj[:, :D_POOL]
    cos = cos_ref[0]
    sin = sin_ref[0]
    o = D_POOL
    cq = _rms(proj[:, o:o + Q_LORA], qn_ref[0]).astype(BF16)
    o += Q_LORA
    ckv = _rms(proj[:, o:o + KV_LORA], kvn_ref[0]).astype(BF16)
    o += KV_LORA
    k_rope = (proj[:, o:o + LANES] * cos + proj[:, o + LANES:o + 2 * LANES] * sin).astype(BF16)
    qall = _dot(cq, wuq_ref[0])
    kvall = _dot(ckv, wukv_ref[0])
    scale = (QK_NOPE + QK_ROPE) ** -0.5 * LOG2_E
    for hd in range(N_MLA_HEADS):
        qb = hd * Q_HEAD_EXT
        q_ref[0, hd, :, :QK_NOPE] = (qall[:, qb:qb + QK_NOPE] * scale).astype(BF16)
        roped = qall[:, qb + QK_NOPE:qb + QK_NOPE + LANES] * cos + qall[:, qb + QK_NOPE + LANES:qb + Q_HEAD_EXT] * sin
        q_ref[0, hd, :, QK_NOPE:] = (roped * scale).astype(BF16)
        kb = hd * (QK_NOPE + V_HEAD)
        k_ref[0, hd, :, :QK_NOPE] = kvall[:, kb:kb + QK_NOPE].astype(BF16)
        k_ref[0, hd, :, QK_NOPE:] = k_rope
        v_ref[0, hd] = kvall[:, kb + QK_NOPE:kb + QK_NOPE + V_HEAD].astype(BF16)


def _mix_in(x, gain, w_in_ext, q_norm, kv_norm, w_uq_ext, w_ukv, cos, sin, l, tm):
    B, S, D = x.shape
    H = N_MLA_HEADS
    lmap = lambda b, i: (l, 0, 0)
    tmap = lambda b, i: (b, i, 0)
    hmap = lambda b, i: (b, 0, i, 0)
    return pl.pallas_call(
        _mix_in_kernel,
        grid=(B, S // tm),
        in_specs=[
            pl.BlockSpec((1, tm, D), tmap),
            pl.BlockSpec((1, 1, D), lmap),
            pl.BlockSpec((1, D, D_IN_EXT), lmap),
            pl.BlockSpec((1, 1, Q_LORA), lmap),
            pl.BlockSpec((1, 1, KV_LORA), lmap),
            pl.BlockSpec((1, Q_LORA, H * Q_HEAD_EXT), lmap),
            pl.BlockSpec((1, KV_LORA, H * (QK_NOPE + V_HEAD)), lmap),
            pl.BlockSpec((1, tm, LANES), tmap),
            pl.BlockSpec((1, tm, LANES), tmap),
        ],
        out_specs=[
            pl.BlockSpec((1, tm, D_POOL), tmap),
            pl.BlockSpec((1, H, tm, QK_PAD), hmap),
            pl.BlockSpec((1, H, tm, QK_PAD), hmap),
            pl.BlockSpec((1, H, tm, V_HEAD), hmap),
        ],
        out_shape=[
            jax.ShapeDtypeStruct((B, S, D_POOL), F32),
            jax.ShapeDtypeStruct((B, H, S, QK_PAD), BF16),
            jax.ShapeDtypeStruct((B, H, S, QK_PAD), BF16),
            jax.ShapeDtypeStruct((B, H, S, V_HEAD), BF16),
        ],
        compiler_params=_cparams("parallel", "parallel"),
        name="mix_in",
    )(x, gain, w_in_ext, q_norm, kv_norm, w_uq_ext, w_ukv, cos, sin)


def _mla_kernel(q_ref, k_ref, v_ref, o_ref, *, S, kc):
    q = q_ref[0, 0]
    m = l = acc = None
    for c in range(S // kc):
        rows = slice(c * kc, (c + 1) * kc)
        s = _dot_nt(q, k_ref[0, 0, rows, :])
        m_c = jnp.max(s, axis=-1, keepdims=True)
        if c == 0:
            m = m_c
            p = jnp.exp2(s - m)
            l = jnp.sum(p, axis=-1, keepdims=True)
            acc = _dot(p.astype(BF16), v_ref[0, 0, rows, :])
        else:
            m_new = jnp.maximum(m, m_c)
            alpha = jnp.exp2(m - m_new)
            p = jnp.exp2(s - m_new)
            l = alpha * l + jnp.sum(p, axis=-1, keepdims=True)
            acc = alpha * acc + _dot(p.astype(BF16), v_ref[0, 0, rows, :])
            m = m_new
    o_ref[0] = (acc / l).astype(BF16)


def _mla_attn(q, k, v, tq, kc):
    B, H, S, _ = q.shape
    return pl.pallas_call(
        functools.partial(_mla_kernel, S=S, kc=kc),
        grid=(B, H, S // tq),
        in_specs=[
            pl.BlockSpec((1, 1, tq, QK_PAD), lambda b, h, i: (b, h, i, 0)),
            pl.BlockSpec((1, 1, S, QK_PAD), lambda b, h, i: (b, h, 0, 0)),
            pl.BlockSpec((1, 1, S, V_HEAD), lambda b, h, i: (b, h, 0, 0)),
        ],
        out_specs=pl.BlockSpec((1, tq, V_HEAD), lambda b, h, i: (b, i, h)),
        out_shape=jax.ShapeDtypeStruct((B, S, H * V_HEAD), BF16),
        compiler_params=_cparams("parallel", "parallel", "parallel"),
        name="mla_attn",
    )(q, k, v)


def _mix_out_kernel(x_ref, u_ref, up_ref, un_ref, wpool_ref, ps_ref, mla_ref, wout_ref,
                    nx_ref, wxq_ref, kv_ref, wxo_ref, nf_ref, wr_ref,
                    x2_ref, hnt_ref, afft_ref, ext_ref, dbl_ref, *, S, tm, rs):
    i = pl.program_id(1)
    nt = pl.num_programs(1)
    hal = POOL_HALO
    ext_ref[0:hal, :] = jnp.where(i > 0, up_ref[0], 0.0)
    ext_ref[hal:hal + tm, :] = u_ref[0]
    ext_ref[hal + tm:2 * hal + tm, :] = jnp.where(i < nt - 1, un_ref[0], 0.0)
    for r0 in range(0, tm, rs):
        rows = slice(r0, r0 + rs)
        t = i * tm + r0 + lax.broadcasted_iota(jnp.int32, (rs, POOL_GROUP), 0)
        pooled = []
        for g, w in enumerate(POOL_WINDOWS):
            left = w // 2
            right = w - 1 - left
            lanes = slice(g * POOL_GROUP, (g + 1) * POOL_GROUP)
            valid = rs + 2 * hal - 1
            dbl_ref[0, pl.ds(0, valid), :] = (ext_ref[pl.ds(r0, valid), lanes]
                                              + ext_ref[pl.ds(r0 + 1, valid), lanes])
            lvl, width = 0, 2
            while width < w:
                valid -= width
                dbl_ref[lvl + 1, pl.ds(0, valid), :] = (dbl_ref[lvl, pl.ds(0, valid), :]
                                                        + dbl_ref[lvl, pl.ds(width, valid), :])
                lvl, width = lvl + 1, 2 * width
            acc = dbl_ref[lvl, pl.ds(hal - left, rs), :]
            cnt = (jnp.clip(t + right + 1, 0, S) - jnp.clip(t - left, 0, S)).astype(F32)
            diff = (acc / cnt - ext_ref[pl.ds(r0 + hal, rs), lanes]).astype(BF16)
            pooled.append(_dot(diff, wpool_ref[0, g]) * ps_ref[0, :, lanes])
        pool_out = jnp.concatenate(pooled, axis=-1).astype(BF16)
        x1 = (x_ref[0, rows, :] + _dot(pool_out, wout_ref[0, :D_POOL, :])
              + _dot(mla_ref[0, rows, :], wout_ref[0, D_POOL:, :]))

        h = _rms(x1, nx_ref[0]).astype(BF16)
        q = (_dot(h, wxq_ref[0]) * (XHEAD ** -0.5)).astype(BF16)
        outs = []
        for hd in range(N_XHEADS):
            cols = slice(hd * XHEAD, (hd + 1) * XHEAD)
            kh = kv_ref[0, 0, :, cols]
            vh = kv_ref[0, 0, :, D_MODEL + hd * XHEAD:D_MODEL + (hd + 1) * XHEAD]
            s = _dot_nt(q[:, cols], kh)
            p = jnp.exp(s - jnp.max(s, axis=-1, keepdims=True))
            denom = jnp.sum(p, axis=-1, keepdims=True)
            outs.append((_dot(p.astype(BF16), vh) / denom).astype(BF16))
        x2 = x1 + _dot(jnp.concatenate(outs, axis=-1), wxo_ref[0])
        x2_ref[0, rows, :] = x2

        hn = _rms(x2, nf_ref[0])
        hi = hn.astype(BF16)
        hnt_ref[0, :, rows] = hi.T
        lo = (hn - hi.astype(F32)).astype(BF16)
        parts = (_dot(hi, wr_ref[0, :D_MODEL, :]) + _dot(lo, wr_ref[0, D_MODEL:, :])).T
        logits = parts[:N_EXPERTS] + parts[N_EXPERTS:2 * N_EXPERTS]
        pe = jnp.exp(logits - jnp.max(logits, axis=0, keepdims=True))
        afft_ref[0, :, rows] = pe / jnp.sum(pe, axis=0, keepdims=True)


def _mix_out(x, u, w_pool, pool_scale, mla, w_out, norm_x, wx_q, kvmem, wx_o, norm_ffn, w_router, l, tm, rs):
    B, S, D = x.shape
    M = kvmem.shape[2]
    nh = tm // POOL_HALO
    last = S // POOL_HALO - 1
    lmap = lambda b, i: (l, 0, 0)
    tmap = lambda b, i: (b, i, 0)
    return pl.pallas_call(
        functools.partial(_mix_out_kernel, S=S, tm=tm, rs=rs),
        grid=(B, S // tm),
        in_specs=[
            pl.BlockSpec((1, tm, D), tmap),
            pl.BlockSpec((1, tm, D_POOL), tmap),
            pl.BlockSpec((1, POOL_HALO, D_POOL), lambda b, i: (b, jnp.maximum(i * nh - 1, 0), 0)),
            pl.BlockSpec((1, POOL_HALO, D_POOL), lambda b, i: (b, jnp.minimum((i + 1) * nh, last), 0)),
            pl.BlockSpec((1, N_POOL_GROUPS, POOL_GROUP, POOL_GROUP), lambda b, i: (l, 0, 0, 0)),
            pl.BlockSpec((1, 1, D_POOL), lmap),
            pl.BlockSpec((1, tm, D_POOL), tmap),
            pl.BlockSpec((1, D, D), lmap),
            pl.BlockSpec((1, 1, D), lmap),
            pl.BlockSpec((1, D, D), lmap),
            pl.BlockSpec((1, 1, M, 2 * D), lambda b, i: (l, b, 0, 0)),
            pl.BlockSpec((1, D, D), lmap),
            pl.BlockSpec((1, 1, D), lmap),
            pl.BlockSpec((1, 2 * D, LANES), lmap),
        ],
        out_specs=[
            pl.BlockSpec((1, tm, D), tmap),
            pl.BlockSpec((1, D, tm), lambda b, i: (b, 0, i)),
            pl.BlockSpec((1, N_EXPERTS, tm), lambda b, i: (b, 0, i)),
        ],
        out_shape=[
            jax.ShapeDtypeStruct((B, S, D), F32),
            jax.ShapeDtypeStruct((B, D, S), BF16),
            jax.ShapeDtypeStruct((B, N_EXPERTS, S), F32),
        ],
        scratch_shapes=[pltpu.VMEM((tm + 2 * POOL_HALO, D_POOL), F32),
                        pltpu.VMEM((4, rs + 2 * POOL_HALO, POOL_GROUP), F32)],
        compiler_params=_cparams("parallel", "parallel"),
        name="mix_out",
    )(x, u, u, u, w_pool, pool_scale, mla, w_out, norm_x, wx_q, kvmem, wx_o, norm_ffn, w_router)


def _select_kernel(afft_ref, post_ref, cnt_ref, *, S, C):
    rt = ROUTE_TILE
    n_tiles = S // rt
    E = N_EXPERTS
    bits = lax.bitcast_convert_type(afft_ref[0], jnp.int32)

    def search(it, prefix):
        cand = prefix | jnp.left_shift(jnp.int32(1), 30 - it)
        cnt = jnp.sum(jnp.where(bits >= cand, 1.0, 0.0), axis=1, keepdims=True)
        return jnp.where(cnt >= C, cand, prefix)

    thr = lax.fori_loop(0, 31, search, jnp.zeros((E, 1), jnp.int32))
    n_gt = jnp.sum(jnp.where(bits > thr, 1.0, 0.0), axis=1, keepdims=True)
    need = C - n_gt
    r = lax.broadcasted_iota(jnp.int32, (rt, rt), 0)
    c = lax.broadcasted_iota(jnp.int32, (rt, rt), 1)
    tri = jnp.where(r <= c, 1.0, 0.0).astype(BF16)
    lane = lax.broadcasted_iota(jnp.int32, (E, LANES), 1)
    counts = jnp.zeros((E, LANES), F32)
    off_eq = jnp.zeros((E, 1), F32)
    off_sel = jnp.zeros((E, 1), F32)
    for kt in range(n_tiles):
        toks = slice(kt * rt, (kt + 1) * rt)
        bc = bits[:, toks]
        gt = bc > thr
        eq = bc == thr
        eqf = jnp.where(eq, 1.0, 0.0)
        rank = off_eq + _dot(eqf.astype(BF16), tri) - eqf
        sel = gt | (eq & (rank < need))
        self_ = jnp.where(sel, 1.0, 0.0)
        slot = off_sel + _dot(self_.astype(BF16), tri) - self_
        post_ref[0, :, toks] = jnp.where(sel, slot, -1.0)
        counts = jnp.where(lane == kt, off_sel, counts)
        off_eq = off_eq + jnp.sum(eqf, axis=1, keepdims=True)
        off_sel = off_sel + jnp.sum(self_, axis=1, keepdims=True)
    cnt_ref[0] = jnp.where(lane == n_tiles, off_sel, counts).astype(jnp.int32)


def _select(afft, C):
    B, E, S = afft.shape
    assert S // ROUTE_TILE < LANES
    return pl.pallas_call(
        functools.partial(_select_kernel, S=S, C=C),
        grid=(B,),
        in_specs=[pl.BlockSpec((1, E, S), lambda b: (b, 0, 0))],
        out_specs=[
            pl.BlockSpec((1, E, S), lambda b: (b, 0, 0)),
            pl.BlockSpec((1, E, LANES), lambda b: (b, 0, 0)),
        ],
        out_shape=[
            jax.ShapeDtypeStruct((B, E, S), F32),
            jax.ShapeDtypeStruct((B, E, LANES), jnp.int32),
        ],
        compiler_params=_cparams("parallel"),
        name="select",
    )(afft)


def _window(first_slot, C):
    return pl.multiple_of(jnp.minimum(first_slot & -LANES, C - ROUTE_TILE), LANES)


def _spills(end_slot, w0, jb):
    return (end_slot > jnp.maximum(w0 + ROUTE_TILE, jb * ROUTE_TILE)) & (w0 < jb * ROUTE_TILE)


def _onehot(prow, start, first=None):
    rel = lax.broadcasted_iota(jnp.int32, (ROUTE_TILE, prow.shape[1]), 0).astype(F32)
    hit = (prow - lax.convert_element_type(start, F32)) == rel
    if first is not None:
        hit = hit & (rel >= lax.convert_element_type(first - start, F32))
    return jnp.where(hit, 1.0, 0.0).astype(BF16)


def _moe_ffn_kernel(cnt_ref, post_ref, hnt_ref, wg_ref, wu_ref, wd_ref, yt_ref,
                    xgt_ref, wg_s, wu_s, wd_s, *, S, C, span):
    rt = ROUTE_TILE
    g = pl.program_id(0)
    b = pl.program_id(1)

    @pl.when(g < N_EXPERTS)
    def _():
        nxt = g & 1
        pr, pd = wg_ref.shape[2], wd_ref.shape[2]
        r0 = pl.multiple_of(b * pr, pr)
        d0 = pl.multiple_of(b * pd, pd)
        wg_s[nxt, pl.ds(r0, pr), :] = wg_ref[0, 0].astype(BF16)
        wu_s[nxt, pl.ds(r0, pr), :] = wu_ref[0, 0].astype(BF16)
        wd_s[nxt, pl.ds(d0, pd), :] = wd_ref[0, 0].astype(BF16)

    @pl.when(g == 0)
    def _():
        yt_ref[0, 0] = jnp.zeros(yt_ref.shape[2:], BF16)

    @pl.when(g > 0)
    def _():
        e = g - 1
        cur = e & 1
        base = (b * N_EXPERTS + e) * (S // rt + 1)
        xgt_ref[...] = jnp.zeros(xgt_ref.shape, BF16)
        n_spans = S // span
        bounds = [cnt_ref[base + t * (span // rt)] for t in range(n_spans + 1)]
        starts = [_window(bounds[t], C) for t in range(n_spans)]
        for t in range(n_spans):
            toks = slice(t * span, (t + 1) * span)
            onehot = _onehot(post_ref[0, 0, :, toks], starts[t])
            xgt_ref[:, pl.ds(starts[t], rt)] += _dot_nt(hnt_ref[0, :, toks], onehot).astype(BF16)
        spills = [(t, jb, _spills(bounds[t + 1], starts[t], jb)) for t in range(n_spans) for jb in range(1, C // rt)]

        @pl.when(functools.reduce(jnp.logical_or, [s for _, _, s in spills], False))
        def _():
            for t, jb, spill in spills:
                @pl.when(spill)
                def _():
                    toks = slice(t * span, (t + 1) * span)
                    onehot = _onehot(post_ref[0, 0, :, toks], jb * rt, first=starts[t] + rt)
                    xgt_ref[:, jb * rt:(jb + 1) * rt] += _dot_nt(hnt_ref[0, :, toks], onehot).astype(BF16)
        x = xgt_ref[...].T
        y = jnp.zeros((C, D_MODEL), F32)
        for c in range(D_EXPERT // FFN_CHUNK):
            cols = slice(c * FFN_CHUNK, (c + 1) * FFN_CHUNK)
            a = _dot(x, wg_s[cur, :, cols])
            gt = _dot(x, wu_s[cur, :, cols])
            y = y + _dot((jax.nn.silu(a) * gt).astype(BF16), wd_s[cur, cols, :])
        yt_ref[0, 0] = y.astype(BF16).T


def _moe_ffn(cnt, post, hnt, w_gate, w_up, w_down, l, C, span):
    B, D, S = hnt.shape
    E = N_EXPERTS
    F = D_EXPERT
    assert D % B == 0 and F % B == 0 and (D // B) % 16 == 0
    wmap = lambda g, b, cnt: (l, jnp.minimum(g, E - 1), b, 0)
    emap = lambda g, b, cnt: (b, jnp.maximum(g - 1, 0), 0, 0)
    return pl.pallas_call(
        functools.partial(_moe_ffn_kernel, S=S, C=C, span=span),
        grid_spec=pltpu.PrefetchScalarGridSpec(
            num_scalar_prefetch=1,
            grid=(E + 1, B),
            in_specs=[
                pl.BlockSpec((1, 1, 1, S), emap),
                pl.BlockSpec((1, D, S), lambda g, b, cnt: (jnp.where(g == 0, 0, b), 0, 0)),
                pl.BlockSpec((1, 1, D // B, F), wmap),
                pl.BlockSpec((1, 1, D // B, F), wmap),
                pl.BlockSpec((1, 1, F // B, D), wmap),
            ],
            out_specs=pl.BlockSpec((1, 1, D, C), lambda g, b, cnt: (b, jnp.where(g == 0, E, g - 1), 0, 0)),
            scratch_shapes=[
                pltpu.VMEM((D, C), BF16),
                pltpu.VMEM((2, D, F), BF16),
                pltpu.VMEM((2, D, F), BF16),
                pltpu.VMEM((2, F, D), BF16),
            ],
        ),
        out_shape=jax.ShapeDtypeStruct((B, E + 1, D, C), BF16),
        compiler_params=_cparams("arbitrary", "arbitrary"),
        name="moe_ffn",
    )(cnt, post, hnt, w_gate, w_up, w_down)


def _combine_kernel(cnt_ref, x_ref, yt_ref, post_ref, afft_ref, gf_ref, o_ref, acc_ref, *, S, C, tt, final):
    rt = ROUTE_TILE
    n_tiles = S // rt
    b = pl.program_id(0)
    ti = pl.program_id(1)
    acc_ref[...] = jnp.zeros(acc_ref.shape, F32)
    blocks = []
    for e in range(N_EXPERTS):
        base = (b * N_EXPERTS + e) * (n_tiles + 1)
        for sub in range(tt // rt):
            tile = ti * (tt // rt) + sub
            blocks.append((e, sub, cnt_ref[base + tile + 1], _window(cnt_ref[base + tile], C)))
    for e, sub, end, w0 in blocks:
        toks = slice(sub * rt, (sub + 1) * rt)
        onehot = _onehot(post_ref[0, e:e + 1, toks], w0)
        gate = afft_ref[0, e:e + 1, toks]
        acc_ref[:, toks] += gate * _dot(yt_ref[0, e, :, pl.ds(w0, rt)], onehot)
    spills = [(blk, jb, _spills(blk[2], blk[3], jb)) for blk in blocks for jb in range(1, C // rt)]

    @pl.when(functools.reduce(jnp.logical_or, [s for _, _, s in spills], False))
    def _():
        for (e, sub, end, w0), jb, spill in spills:
            @pl.when(spill)
            def _():
                toks = slice(sub * rt, (sub + 1) * rt)
                onehot = _onehot(post_ref[0, e:e + 1, toks], jb * rt, first=w0 + rt)
                gate = afft_ref[0, e:e + 1, toks]
                acc_ref[:, toks] += gate * _dot(yt_ref[0, e, :, jb * rt:(jb + 1) * rt], onehot)

    out = x_ref[0] + acc_ref[...].T
    if final:
        out = _rms(out, gf_ref[0])
    o_ref[0] = out


def _combine(cnt, x, yt, post, afft, norm_final, C, tt, final):
    B, S, D = x.shape
    E = N_EXPERTS
    return pl.pallas_call(
        functools.partial(_combine_kernel, S=S, C=C, tt=tt, final=final),
        grid_spec=pltpu.PrefetchScalarGridSpec(
            num_scalar_prefetch=1,
            grid=(B, S // tt),
            in_specs=[
                pl.BlockSpec((1, tt, D), lambda b, t, cnt: (b, t, 0)),
                pl.BlockSpec((1, E, D, C), lambda b, t, cnt: (b, 0, 0, 0)),
                pl.BlockSpec((1, E, tt), lambda b, t, cnt: (b, 0, t)),
                pl.BlockSpec((1, E, tt), lambda b, t, cnt: (b, 0, t)),
                pl.BlockSpec((1, 1, D), lambda b, t, cnt: (0, 0, 0)),
            ],
            out_specs=pl.BlockSpec((1, tt, D), lambda b, t, cnt: (b, t, 0)),
            scratch_shapes=[pltpu.VMEM((D, tt), F32)],
        ),
        out_shape=jax.ShapeDtypeStruct((B, S, D), F32),
        compiler_params=_cparams("parallel", "parallel"),
        name="combine",
    )(cnt, x, yt, post, afft, norm_final)


def _rot_cols(w):
    half = w.shape[-1] // 2
    return jnp.concatenate([-w[..., half:], w[..., :half]], axis=-1)


def _pad_lanes(w):
    return jnp.concatenate([w, jnp.zeros(w.shape[:-1] + (LANES - w.shape[-1],), w.dtype)], axis=-1)


def kernel(x, mem, positions, norm_mix, w_in, q_norm, kv_norm, w_uq, w_ukv, w_pool, pool_scale,
           w_out, norm_x, mem_norm, wx_q, wx_kv, wx_o, norm_ffn, w_router, w_gate, w_up, w_down,
           norm_final):
    B, S, D = x.shape
    L = w_in.shape[0]
    C = CAPACITY_FACTOR * S // N_EXPERTS
    assert D == D_MODEL and S % 1024 == 0 and C % ROUTE_TILE == 0
    tm = 1024
    tmo = 1024
    rs = 256
    tq = 1024
    kc = 1024
    tt = 512
    span = 512

    half = QK_ROPE // 2
    inv_freq = ROPE_THETA ** (-jnp.arange(half, dtype=F32) / half)
    ang = positions.astype(F32)[..., None] * inv_freq
    cos = _pad_lanes(jnp.concatenate([jnp.cos(ang)] * 2, axis=-1))
    sin = _pad_lanes(jnp.concatenate([jnp.sin(ang)] * 2, axis=-1))

    k_r = w_in[..., D_IN - QK_ROPE:]
    w_in_ext = jnp.concatenate(
        [w_in[..., :D_IN - QK_ROPE], _pad_lanes(k_r), _pad_lanes(_rot_cols(k_r))], axis=-1).astype(BF16)
    wq = w_uq.reshape(L, Q_LORA, N_MLA_HEADS, QK_NOPE + QK_ROPE)
    q_r = wq[..., QK_NOPE:]
    w_uq_ext = jnp.concatenate([wq[..., :QK_NOPE], _pad_lanes(q_r), _pad_lanes(_rot_cols(q_r))], axis=-1)
    w_uq_ext = w_uq_ext.reshape(L, Q_LORA, N_MLA_HEADS * Q_HEAD_EXT).astype(BF16)
    wr_hi = w_router.astype(BF16)
    wr_lo = (w_router - wr_hi.astype(F32)).astype(BF16)
    w_router_p = jnp.concatenate(
        [_pad_lanes(jnp.concatenate([wr_hi, wr_lo], axis=-1)), _pad_lanes(wr_hi)], axis=1)
    row = lambda g: g.reshape(L, 1, g.shape[-1])
    norm_mix, q_norm, kv_norm, pool_scale, norm_x, mem_norm, norm_ffn = map(
        row, (norm_mix, q_norm, kv_norm, pool_scale, norm_x, mem_norm, norm_ffn))
    norm_final = norm_final.reshape(1, 1, D)
    w_ukv, w_pool, w_out, wx_q, wx_kv, wx_o = (
        w.astype(BF16) for w in (w_ukv, w_pool, w_out, wx_q, wx_kv, wx_o))

    kvmem = _memkv(mem, mem_norm, wx_kv)

    n_t1 = S // ROUTE_TILE + 1
    for l in range(L):
        u, q, k, v = _mix_in(x, norm_mix, w_in_ext, q_norm, kv_norm, w_uq_ext, w_ukv, cos, sin, l, tm)
        mla = _mla_attn(q, k, v, tq, kc)
        x, hnt, afft = _mix_out(x, u, w_pool, pool_scale, mla, w_out, norm_x, wx_q, kvmem, wx_o,
                                norm_ffn, w_router_p, l, tmo, rs)
        post, cnt = _select(afft, C)
        cnt = cnt[:, :, :n_t1].reshape(-1)
        post_e = post.reshape(B, N_EXPERTS, 1, S)
        yt = _moe_ffn(cnt, post_e, hnt, w_gate, w_up, w_down, l, C, span)
        x = _combine(cnt, x, yt, post, afft, norm_final, C, tt, final=(l == L - 1))
    return x
```
